```python
import math
import jax, jax.numpy as jnp
from jax import lax
import numpy as np

D_MODEL = 1024
BATCH = 2
SEQ = 8192
DEPTH = 1

GRID_W = 64
HEAD_DIM = 64
NA_HEADS = 8
NA_WIDTH = NA_HEADS * HEAD_DIM
NA_KR_MAX = 8
NA_KC = 16
DA_HEADS = 4
DA_VDIM = 2 * HEAD_DIM
DA_QK = DA_HEADS * 2 * HEAD_DIM
DA_WIDTH = DA_HEADS * DA_VDIM
MIX_WIDTH = NA_WIDTH + DA_WIDTH
IN_COLS = 3 * NA_WIDTH + 2 * DA_QK + DA_WIDTH
D_FF = -(-8 * D_MODEL // (3 * 256)) * 256
ROPE_THETA = 10000.0
LN_EPS = 1e-5
BLOCK_Q = 128
ALPHA = (2.0 * DEPTH) ** 0.25
BETA = (8.0 * DEPTH) ** -0.25

kernel_name = "hybrid_natten_diffattn_deepnorm_adaln_encoder"


def lambda_init_for(layer_idx):
    return 0.8 - 0.6 * math.exp(-0.3 * (layer_idx - 1))


def layer_norm(x, g=None, b=None):
    xf = x.astype(jnp.float32)
    mu = jnp.mean(xf, axis=-1, keepdims=True)
    var = jnp.mean(jnp.square(xf - mu), axis=-1, keepdims=True)
    y = (xf - mu) * lax.rsqrt(var + LN_EPS)
    if g is not None:
        y = y * g.astype(jnp.float32) + b.astype(jnp.float32)
    return y.astype(x.dtype)


def rms_norm(x, g):
    xf = x.astype(jnp.float32)
    y = xf * lax.rsqrt(jnp.mean(jnp.square(xf), axis=-1, keepdims=True) + LN_EPS)
    return (y * g.astype(jnp.float32)).astype(x.dtype)


def rope(x):
    S_ = x.shape[1]
    half = HEAD_DIM // 2
    freqs = 1.0 / (ROPE_THETA ** (jnp.arange(half, dtype=jnp.float32) / half))
    ang = jnp.arange(S_, dtype=jnp.float32)[:, None] * freqs[None, :]
    cos = jnp.cos(ang)[None, :, None, None, :]
    sin = jnp.sin(ang)[None, :, None, None, :]
    xf = x.astype(jnp.float32)
    x1, x2 = xf[..., :half], xf[..., half:]
    return jnp.concatenate([x1 * cos - x2 * sin, x2 * cos + x1 * sin], axis=-1).astype(x.dtype)


def neighbourhood_attention(q, k, v, rpb):
    B_, S_ = q.shape[0], q.shape[1]
    rows = S_ // GRID_W
    kr = min(NA_KR_MAX, rows)
    scale = HEAD_DIM ** -0.5

    def to_grid(t):
        return t.reshape(B_, rows, GRID_W, NA_HEADS, HEAD_DIM).transpose(1, 0, 3, 2, 4)

    q_g, k_g, v_g = to_grid(q), to_grid(k), to_grid(v)
    col = np.arange(GRID_W)
    col_start = np.clip(col - NA_KC // 2, 0, GRID_W - NA_KC)
    col_idx = col_start[:, None] + np.arange(NA_KC)[None, :]
    col_rel = col_idx - col[:, None] + (NA_KC - 1)
    rpb_cols = rpb[:, :, col_rel]

    def one_row(args):
        q_row, r = args
        rs = jnp.clip(r - kr // 2, 0, rows - kr)
        k_rows = lax.dynamic_slice_in_dim(k_g, rs, kr, axis=0)
        v_rows = lax.dynamic_slice_in_dim(v_g, rs, kr, axis=0)
        k_win = k_rows[:, :, :, col_idx]
        v_win = v_rows[:, :, :, col_idx]
        row_rel = rs + jnp.arange(kr) - r + (NA_KR_MAX - 1)
        bias = rpb_cols[:, row_rel].transpose(0, 2, 1, 3)
        s = jnp.einsum('bhwd,rbhwkd->bhwrk', q_row, k_win).astype(jnp.float32) * scale
        s = s + bias[None].astype(jnp.float32)
        p = jax.nn.softmax(s.reshape(B_, NA_HEADS, GRID_W, kr * NA_KC), axis=-1)
        p = p.reshape(B_, NA_HEADS, GRID_W, kr, NA_KC).astype(v.dtype)
        return jnp.einsum('bhwrk,rbhwkd->bhwd', p, v_win)

    out = lax.map(one_row, (q_g, jnp.arange(rows)))
    return out.transpose(1, 0, 3, 2, 4).reshape(B_, S_, NA_WIDTH)


def differential_attention(q, k, v, lam, subln_g, lambda_init):
    B_, S_ = q.shape[0], q.shape[1]
    nblk = S_ // BLOCK_Q
    scale = HEAD_DIM ** -0.5
    qb = q.reshape(B_, nblk, BLOCK_Q, DA_HEADS, 2, HEAD_DIM).transpose(1, 0, 3, 4, 2, 5)
    kt = k.transpose(0, 2, 3, 1, 4)
    vt = v.transpose(0, 2, 1, 3)

    def one_block(q_blk):
        s = jnp.einsum('bhiqd,bhikd->bhiqk', q_blk, kt).astype(jnp.float32) * scale
        p = jax.nn.softmax(s, axis=-1)
        a = p[:, :, 0] - lam * p[:, :, 1]
        return jnp.einsum('bhqk,bhke->bhqe', a.astype(vt.dtype), vt)

    o = lax.map(one_block, qb)
    o = o.transpose(1, 0, 3, 2, 4).reshape(B_, S_, DA_HEADS, DA_VDIM)
    o = rms_norm(o, subln_g) * (1.0 - lambda_init)
    return o.reshape(B_, S_, DA_WIDTH)


def setup_inputs(seed: int = 0) -> dict:
    key = jax.random.key(seed)
    ks = jax.random.split(key, 20)
    f32 = jnp.float32
    n = lambda k, shape, s: jax.random.normal(k, shape, f32) * s
    return {
        "x": n(ks[0], (BATCH, SEQ, D_MODEL), 1.0),
        "c": n(ks[1], (BATCH, D_MODEL), 1.0),
        "w_ada": n(ks[2], (DEPTH, D_MODEL, 6 * D_MODEL), 0.1 * D_MODEL ** -0.5),
        "b_ada": n(ks[3], (DEPTH, 6 * D_MODEL), 0.01),
        "w_in": n(ks[4], (DEPTH, D_MODEL, IN_COLS), D_MODEL ** -0.5),
        "rpb": n(ks[5], (DEPTH, NA_HEADS, 2 * NA_KR_MAX - 1, 2 * NA_KC - 1), 0.1),
        "lambda_q1": n(ks[6], (DEPTH, HEAD_DIM), 0.1),
        "lambda_k1": n(ks[7], (DEPTH, HEAD_DIM), 0.1),
        "lambda_q2": n(ks[8], (DEPTH, HEAD_DIM), 0.1),
        "lambda_k2": n(ks[9], (DEPTH, HEAD_DIM), 0.1),
        "subln_g": 1.0 + n(ks[10], (DEPTH, DA_VDIM), 0.02),
        "w_out": n(ks[11], (DEPTH, MIX_WIDTH, D_MODEL), BETA * MIX_WIDTH ** -0.5),
        "ln1_g": 1.0 + n(ks[12], (DEPTH, D_MODEL), 0.02),
        "ln1_b": n(ks[13], (DEPTH, D_MODEL), 0.02),
        "w_gate": n(ks[14], (DEPTH, D_MODEL, D_FF), D_MODEL ** -0.5),
        "w_up": n(ks[15], (DEPTH, D_MODEL, D_FF), D_MODEL ** -0.5),
        "w_down": n(ks[16], (DEPTH, D_FF, D_MODEL), BETA * D_FF ** -0.5),
        "ln2_g": 1.0 + n(ks[17], (DEPTH, D_MODEL), 0.02),
        "ln2_b": n(ks[18], (DEPTH, D_MODEL), 0.02),
    }


def reference(x, c, w_ada, b_ada, w_in, rpb, lambda_q1, lambda_k1, lambda_q2, lambda_k2,
              subln_g, w_out, ln1_g, ln1_b, w_gate, w_up, w_down, ln2_g, ln2_b):
    B_, S_, _ = x.shape
    c_act = jax.nn.silu(c)
    for l in range(DEPTH):
        lambda_init = lambda_init_for(l + 1)
        mod = c_act @ w_ada[l] + b_ada[l]
        shift1, scale1, gate1, shift2, scale2, gate2 = [m[:, None, :] for m in jnp.split(mod, 6, axis=-1)]

        h = layer_norm(x) * (1.0 + scale1) + shift1
        proj = h @ w_in[l]
        o1 = NA_WIDTH; o2 = 2 * NA_WIDTH; o3 = 3 * NA_WIDTH
        o4 = o3 + DA_QK; o5 = o4 + DA_QK
        na_q = proj[..., :o1].reshape(B_, S_, NA_HEADS, HEAD_DIM)
        na_k = proj[..., o1:o2].reshape(B_, S_, NA_HEADS, HEAD_DIM)
        na_v = proj[..., o2:o3].reshape(B_, S_, NA_HEADS, HEAD_DIM)
        da_q = rope(proj[..., o3:o4].reshape(B_, S_, DA_HEADS, 2, HEAD_DIM))
        da_k = rope(proj[..., o4:o5].reshape(B_, S_, DA_HEADS, 2, HEAD_DIM))
        da_v = proj[..., o5:].reshape(B_, S_, DA_HEADS, DA_VDIM)

        out_a = neighbourhood_attention(na_q, na_k, na_v, rpb[l])
        lam = (jnp.exp(jnp.sum(lambda_q1[l].astype(jnp.float32) * lambda_k1[l].astype(jnp.float32)))
               - jnp.exp(jnp.sum(lambda_q2[l].astype(jnp.float32) * lambda_k2[l].astype(jnp.float32)))
               + lambda_init)
        out_b = differential_attention(da_q, da_k, da_v, lam, subln_g[l], lambda_init)
        mix = jnp.concatenate([out_a, out_b], axis=-1) @ w_out[l]
        x = layer_norm(ALPHA * x + (1.0 + gate1) * mix, ln1_g[l], ln1_b[l])

        h = layer_norm(x) * (1.0 + scale2) + shift2
        ffn = (jax.nn.silu(h @ w_gate[l]) * (h @ w_up[l])) @ w_down[l]
        x = layer_norm(ALPHA * x + (1.0 + gate2) * ffn, ln2_g[l], ln2_b[l])
    return x
```

```python
import functools
import math

import jax
import jax.numpy as jnp
from jax import lax
from jax.experimental import pallas as pl
from jax.experimental.pallas import tpu as pltpu

D_MODEL = 1024
GRID_W = 64
HEAD_DIM = 64
NA_HEADS = 8
NA_WIDTH = NA_HEADS * HEAD_DIM
NA_KR = 8
NA_KC = 16
DA_HEADS = 4
DA_VDIM = 2 * HEAD_DIM
DA_QK = DA_HEADS * 2 * HEAD_DIM
DA_WIDTH = DA_HEADS * DA_VDIM
IN_COLS = 3 * NA_WIDTH + 2 * DA_QK + DA_WIDTH
ROPE_THETA = 10000.0
LN_EPS = 1e-5
DEPTH = 1
ALPHA = (2.0 * DEPTH) ** 0.25
LAMBDA_INIT = 0.8 - 0.6 * math.exp(-0.3 * 0)
QK_SCALE = HEAD_DIM ** -0.5

LANES = 128
V7X_VMEM_BYTES = 64 * 1024 * 1024
NEG_BIG = -1e30

BF16 = jnp.bfloat16
F32 = jnp.float32

TM = 512
TQ = 512
TK = 512
ADA_TN = 1536
VMEM_LIMIT = 56 * 1024 * 1024


def _cparams(sem):
    return pltpu.CompilerParams(dimension_semantics=sem, vmem_limit_bytes=VMEM_LIMIT)


def _ln(x):
    mu = jnp.mean(x, axis=-1, keepdims=True)
    xc = x - mu
    var = jnp.mean(xc * xc, axis=-1, keepdims=True)
    return xc * lax.rsqrt(var + LN_EPS)


def _silu(x):
    return x / (1.0 + jnp.exp(-x))


def _ada_kernel(c_ref, w_ref, b_ref, o_ref):
    act = _silu(c_ref[...])
    o_ref[...] = jnp.dot(act, w_ref[...], preferred_element_type=F32) + b_ref[...]


def _ada(c_pad, w, b):
    rows, d = c_pad.shape
    n = w.shape[1]
    return pl.pallas_call(
        _ada_kernel,
        grid=(n // ADA_TN,),
        in_specs=[
            pl.BlockSpec((rows, d), lambda j: (0, 0)),
            pl.BlockSpec((d, ADA_TN), lambda j: (0, j)),
            pl.BlockSpec((1, ADA_TN), lambda j: (0, j)),
        ],
        out_specs=pl.BlockSpec((rows, ADA_TN), lambda j: (0, j)),
        out_shape=jax.ShapeDtypeStruct((rows, n), F32),
        compiler_params=_cparams(("arbitrary",)),
        name="ada",
    )(c_pad, w, b)


def _inv_freq_lanes(shape):
    lane = lax.broadcasted_iota(jnp.int32, shape, len(shape) - 1)
    f = (lane & (HEAD_DIM // 2 - 1)).astype(F32)
    return jnp.exp(f * (-math.log(ROPE_THETA) / (HEAD_DIM // 2)))


def _rope_tab_kernel(cos_ref, sin_ref):
    r = lax.broadcasted_iota(jnp.int32, (TM, LANES), 0).astype(F32)
    ang = r * _inv_freq_lanes((TM, LANES))
    cos_ref[...] = jnp.cos(ang)
    sin_ref[...] = jnp.sin(ang)


def _rope_tab():
    return pl.pallas_call(
        _rope_tab_kernel,
        out_shape=(jax.ShapeDtypeStruct((TM, LANES), F32),) * 2,
        name="rope_tab",
    )()


def _rpb_tab_kernel(rpb_ref, o_ref):
    rr = pl.program_id(0)
    c = lax.broadcasted_iota(jnp.int32, (GRID_W, LANES), 0)
    lane = lax.broadcasted_iota(jnp.int32, (GRID_W, LANES), 1)
    w = lane & (GRID_W - 1)
    first_head = lane < GRID_W
    t = c - w + (NA_KC - 1)
    cs = jnp.clip(w - NA_KC // 2, 0, GRID_W - NA_KC)
    off = c - cs
    n_rel = 2 * NA_KC - 1
    n_row = 2 * NA_KR - 1
    for p in range(NA_HEADS // 2):
        acc = jnp.zeros((GRID_W, LANES), F32)
        for tt in range(n_rel):
            va = rpb_ref[((2 * p) * n_row + rr) * n_rel + tt]
            vb = rpb_ref[((2 * p + 1) * n_row + rr) * n_rel + tt]
            acc = jnp.where(t == tt, jnp.where(first_head, va, vb), acc)
        acc = jnp.where(off >= 0, acc, NEG_BIG)
        o_ref[p] = jnp.where(off < NA_KC, acc, NEG_BIG)


def _rpb_tab(rpb_flat):
    n_row = 2 * NA_KR - 1
    return pl.pallas_call(
        _rpb_tab_kernel,
        grid=(n_row,),
        in_specs=[pl.BlockSpec(memory_space=pltpu.SMEM)],
        out_specs=pl.BlockSpec((None, NA_HEADS // 2, GRID_W, LANES), lambda r: (r, 0, 0, 0)),
        out_shape=jax.ShapeDtypeStruct((n_row, NA_HEADS // 2, GRID_W, LANES), F32),
        compiler_params=_cparams(("arbitrary",)),
        name="rpb_tab",
    )(rpb_flat)


def _rope(x, cosv, sin_signed, first_half):
    swapped = jnp.where(first_half, pltpu.roll(x, LANES - HEAD_DIM // 2, 1),
                        pltpu.roll(x, HEAD_DIM // 2, 1))
    return x * cosv + swapped * sin_signed


def _inproj_kernel(x_ref, mod_ref, w_ref, cosb_ref, sinb_ref,
                   naq_ref, nak_ref, nav_ref, daq_ref, dak_ref, davt_ref):
    i = pl.program_id(1)
    shift1 = mod_ref[:, 0:D_MODEL]
    scale1 = mod_ref[:, D_MODEL:2 * D_MODEL]
    h = _ln(x_ref[...]) * (1.0 + scale1) + shift1
    proj = jnp.dot(h.astype(BF16), w_ref[...], preferred_element_type=F32)

    o1, o2, o3 = NA_WIDTH, 2 * NA_WIDTH, 3 * NA_WIDTH
    o4, o5 = o3 + DA_QK, o3 + 2 * DA_QK
    naq_ref[...] = (proj[:, :o1] * QK_SCALE).astype(BF16)
    nak_ref[...] = proj[:, o1:o2].astype(BF16)
    nav_ref[...] = proj[:, o2:o3].astype(BF16)

    base = (i * TM).astype(F32) * _inv_freq_lanes((1, LANES))
    cosa, sina = jnp.cos(base), jnp.sin(base)
    cosb, sinb = cosb_ref[...], sinb_ref[...]
    cosv = cosa * cosb - sina * sinb
    sinv = sina * cosb + cosa * sinb
    lane = lax.broadcasted_iota(jnp.int32, (TM, LANES), 1)
    first_half = (lane & (HEAD_DIM - 1)) < HEAD_DIM // 2
    sin_signed = jnp.where(first_half, -sinv, sinv)
    for g in range(DA_QK // LANES):
        sl = slice(g * LANES, (g + 1) * LANES)
        q = _rope(proj[:, o3 + g * LANES:o3 + (g + 1) * LANES], cosv, sin_signed, first_half)
        daq_ref[:, sl] = (q * QK_SCALE).astype(BF16)
        k = _rope(proj[:, o4 + g * LANES:o4 + (g + 1) * LANES], cosv, sin_signed, first_half)
        dak_ref[:, sl] = k.astype(BF16)
    davt_ref[...] = proj[:, o5:].T.astype(BF16)


def _inproj(x, mod3, w_in, cosb, sinb):
    b, s, d = x.shape
    tok = lambda width: pl.BlockSpec((None, TM, width), lambda bi, i: (bi, i, 0))
    sds = lambda width: jax.ShapeDtypeStruct((b, s, width), BF16)
    return pl.pallas_call(
        _inproj_kernel,
        grid=(b, s // TM),
        in_specs=[
            tok(d),
            pl.BlockSpec((None, 1, mod3.shape[-1]), lambda bi, i: (bi, 0, 0)),
            pl.BlockSpec((d, IN_COLS), lambda bi, i: (0, 0)),
            pl.BlockSpec((TM, LANES), lambda bi, i: (0, 0)),
            pl.BlockSpec((TM, LANES), lambda bi, i: (0, 0)),
        ],
        out_specs=[tok(NA_WIDTH), tok(NA_WIDTH), tok(NA_WIDTH), tok(DA_QK), tok(DA_QK),
                   pl.BlockSpec((None, DA_WIDTH, TM), lambda bi, i: (bi, 0, i))],
        out_shape=[sds(NA_WIDTH), sds(NA_WIDTH), sds(NA_WIDTH), sds(DA_QK), sds(DA_QK),
                   jax.ShapeDtypeStruct((b, DA_WIDTH, s), BF16)],
        compiler_params=_cparams(("arbitrary", "arbitrary")),
        name="inproj",
    )(x, mod3, w_in, cosb, sinb)


def _nattn_kernel(q_ref, k_ref, v_ref, t_ref, o_ref, *, rows):
    r = pl.program_id(1)
    rs = jnp.clip(r - NA_KR // 2, 0, rows - NA_KR)
    d = r - rs
    n_keys = NA_KR * GRID_W
    start = pl.multiple_of(rs * GRID_W, GRID_W)
    lane = lax.broadcasted_iota(jnp.int32, (GRID_W, LANES), 1)
    first_head = lane < GRID_W
    for p in range(NA_HEADS // 2):
        sl = slice(p * LANES, (p + 1) * LANES)
        q2 = q_ref[:, sl]
        zero = jnp.zeros_like(q2)
        rhs = jnp.concatenate([jnp.where(first_head, q2, zero),
                               jnp.where(first_head, zero, q2)], axis=0)
        k2 = k_ref[pl.ds(start, n_keys), sl]
        v2 = v_ref[pl.ds(start, n_keys), sl]
        st = lax.dot_general(k2, rhs, (((1,), (1,)), ((), ())),
                             preferred_element_type=F32)
        st = st + t_ref[pl.ds(NA_KR - 1 - d, NA_KR), p].reshape(n_keys, LANES)
        m = jnp.max(st, axis=0, keepdims=True)
        e = jnp.exp(st - m)
        l = jnp.sum(e, axis=0, keepdims=True)
        pn = (e * (1.0 / l)).T.astype(BF16)
        o2 = jnp.dot(pn, v2, preferred_element_type=F32)
        o_ref[:, sl] = jnp.where(first_head, o2[:GRID_W], o2[GRID_W:]).astype(o_ref.dtype)


def _nattn(q, k, v, tab):
    b, s, width = q.shape
    rows = s // GRID_W
    full = pl.BlockSpec((None, s, width), lambda bi, r: (bi, 0, 0))
    row = pl.BlockSpec((None, GRID_W, width), lambda bi, r: (bi, r, 0))
    return pl.pallas_call(
        functools.partial(_nattn_kernel, rows=rows),
        grid=(b, rows),
        in_specs=[row, full, full,
                  pl.BlockSpec(tab.shape, lambda bi, r: (0, 0, 0, 0))],
        out_specs=row,
        out_shape=jax.ShapeDtypeStruct((b, s, width), BF16),
        compiler_params=_cparams(("arbitrary", "arbitrary")),
        name="nattn",
    )(q, k, v, tab)


def _dattn_kernel(q_ref, k_ref, vt_ref, lam_ref, g_ref, o_ref, acc_ref, *, n_kb):
    q = q_ref[...]
    lane = lax.broadcasted_iota(jnp.int32, q.shape, 1)
    zero = jnp.zeros_like(q)
    rhs = (jnp.where(lane < HEAD_DIM, q, zero), jnp.where(lane < HEAD_DIM, zero, q))
    acc_ref[...] = jnp.zeros_like(acc_ref)

    def body(j, carry):
        start = pl.multiple_of(j * TK, TK)
        kb = k_ref[pl.ds(start, TK), :]
        vb = vt_ref[:, pl.ds(start, TK)]
        new = []
        for mp in range(2):
            m_old, l_old = carry[2 * mp], carry[2 * mp + 1]
            s = lax.dot_general(kb, rhs[mp], (((1,), (1,)), ((), ())),
                                preferred_element_type=F32)
            m_new = jnp.maximum(m_old, jnp.max(s, axis=0, keepdims=True))
            a = jnp.exp(m_old - m_new)
            p = jnp.exp(s - m_new)
            l_new = a * l_old + jnp.sum(p, axis=0, keepdims=True)
            acc_ref[mp] = acc_ref[mp] * a + jnp.dot(vb, p.astype(BF16),
                                                    preferred_element_type=F32)
            new += [m_new, l_new]
        return tuple(new)

    m0 = jnp.full((1, TQ), -jnp.inf, F32)
    l0 = jnp.zeros((1, TQ), F32)
    _, l1, _, l2 = lax.fori_loop(0, n_kb, body, (m0, l0, m0, l0))

    lam = (jnp.exp(jnp.sum(lam_ref[0:1, :] * lam_ref[1:2, :], axis=-1, keepdims=True))
           - jnp.exp(jnp.sum(lam_ref[2:3, :] * lam_ref[3:4, :], axis=-1, keepdims=True))
           + LAMBDA_INIT)
    o = acc_ref[0] * (1.0 / l1) - lam * (acc_ref[1] * (1.0 / l2))
    ms = jnp.mean(o * o, axis=0, keepdims=True)
    o = o * lax.rsqrt(ms + LN_EPS)
    o_ref[...] = (o.T * g_ref[...] * (1.0 - LAMBDA_INIT)).astype(o_ref.dtype)


def _dattn(q, k, vt, lam4, g):
    b, s, _ = q.shape
    return pl.pallas_call(
        functools.partial(_dattn_kernel, n_kb=s // TK),
        grid=(b, DA_HEADS, s // TQ),
        in_specs=[
            pl.BlockSpec((None, TQ, LANES), lambda bi, h, i: (bi, i, h)),
            pl.BlockSpec((None, s, LANES), lambda bi, h, i: (bi, 0, h)),
            pl.BlockSpec((None, DA_VDIM, s), lambda bi, h, i: (bi, h, 0)),
            pl.BlockSpec(lam4.shape, lambda bi, h, i: (0, 0)),
            pl.BlockSpec(g.shape, lambda bi, h, i: (0, 0)),
        ],
        out_specs=pl.BlockSpec((None, TQ, DA_VDIM), lambda bi, h, i: (bi, i, h)),
        out_shape=jax.ShapeDtypeStruct((b, s, DA_WIDTH), BF16),
        scratch_shapes=[pltpu.VMEM((2, DA_VDIM, TQ), F32)],
        compiler_params=_cparams(("arbitrary", "arbitrary", "arbitrary")),
        name="dattn",
    )(q, k, vt, lam4, g)


def _outproj_kernel(oa_ref, ob_ref, w_ref, x_ref, mod_ref, g_ref, b_ref, x1_ref, h2_ref):
    mix = (jnp.dot(oa_ref[...], w_ref[0:NA_WIDTH, :], preferred_element_type=F32)
           + jnp.dot(ob_ref[...], w_ref[NA_WIDTH:, :], preferred_element_type=F32))
    gate1 = mod_ref[:, 2 * D_MODEL:3 * D_MODEL]
    shift2 = mod_ref[:, 3 * D_MODEL:4 * D_MODEL]
    scale2 = mod_ref[:, 4 * D_MODEL:5 * D_MODEL]
    x1 = _ln(ALPHA * x_ref[...] + (1.0 + gate1) * mix) * g_ref[...] + b_ref[...]
    x1_ref[...] = x1
    h2_ref[...] = (_ln(x1) * (1.0 + scale2) + shift2).astype(h2_ref.dtype)


def _outproj(oa, ob, w_out, x, mod3, g, bb):
    b, s, d = x.shape
    tok = lambda width: pl.BlockSpec((None, TM, width), lambda bi, i: (bi, i, 0))
    vec = pl.BlockSpec((1, d), lambda bi, i: (0, 0))
    return pl.pallas_call(
        _outproj_kernel,
        grid=(b, s // TM),
        in_specs=[tok(NA_WIDTH), tok(DA_WIDTH),
                  pl.BlockSpec(w_out.shape, lambda bi, i: (0, 0)),
                  tok(d),
                  pl.BlockSpec((None, 1, mod3.shape[-1]), lambda bi, i: (bi, 0, 0)),
                  vec, vec],
        out_specs=[tok(d), tok(d)],
        out_shape=[jax.ShapeDtypeStruct((b, s, d), F32), jax.ShapeDtypeStruct((b, s, d), BF16)],
        compiler_params=_cparams(("arbitrary", "arbitrary")),
        name="outproj",
    )(oa, ob, w_out, x, mod3, g, bb)


def _ffn_kernel(h_ref, x1_ref, wg_ref, wu_ref, wd_ref, mod_ref, g_ref, b_ref, o_ref):
    h = h_ref[...]
    gate = jnp.dot(h, wg_ref[...], preferred_element_type=F32)
    up = jnp.dot(h, wu_ref[...], preferred_element_type=F32)
    act = (_silu(gate) * up).astype(BF16)
    ffn = jnp.dot(act, wd_ref[...], preferred_element_type=F32)
    gate2 = mod_ref[:, 5 * D_MODEL:6 * D_MODEL]
    y = ALPHA * x1_ref[...] + (1.0 + gate2) * ffn
    o_ref[...] = _ln(y) * g_ref[...] + b_ref[...]


def _ffn(h2, x1, wg, wu, wd, mod3, g, bb):
    b, s, d = x1.shape
    tok = pl.BlockSpec((None, TM, d), lambda bi, i: (bi, i, 0))
    vec = pl.BlockSpec((1, d), lambda bi, i: (0, 0))
    once = lambda shape: pl.BlockSpec(shape, lambda bi, i: (0, 0), pipeline_mode=pl.Buffered(1))
    return pl.pallas_call(
        _ffn_kernel,
        grid=(b, s // TM),
        in_specs=[tok, tok, once(wg.shape), once(wu.shape), once(wd.shape),
                  pl.BlockSpec((None, 1, mod3.shape[-1]), lambda bi, i: (bi, 0, 0)),
                  vec, vec],
        out_specs=tok,
        out_shape=jax.ShapeDtypeStruct((b, s, d), F32),
        compiler_params=_cparams(("arbitrary", "arbitrary")),
        name="ffn",
    )(h2, x1, wg, wu, wd, mod3, g, bb)


def kernel(x, c, w_ada, b_ada, w_in, rpb, lambda_q1, lambda_k1, lambda_q2, lambda_k2,
           subln_g, w_out, ln1_g, ln1_b, w_gate, w_up, w_down, ln2_g, ln2_b):
    b, s, d = x.shape
    assert d == D_MODEL and s % TM == 0 and s % TQ == 0 and s % TK == 0 and s % GRID_W == 0
    assert w_ada.shape[0] == DEPTH == 1
    l = 0
    sub = 8
    c_pad = jnp.zeros((sub, d), F32).at[:b].set(c)
    mod = _ada(c_pad, w_ada[l], b_ada[l].reshape(1, -1))[:b]
    mod3 = mod.reshape(b, 1, -1)

    cosb, sinb = _rope_tab()
    tab = _rpb_tab(rpb[l].reshape(-1))

    naq, nak, nav, daq, dak, davt = _inproj(x, mod3, w_in[l].astype(BF16), cosb, sinb)
    out_a = _nattn(naq, nak, nav, tab)
    lam4 = jnp.stack([lambda_q1[l], lambda_k1[l], lambda_q2[l], lambda_k2[l]])
    out_b = _dattn(daq, dak, davt, lam4, subln_g[l].reshape(1, -1))
    x1, h2 = _outproj(out_a, out_b, w_out[l].astype(BF16), x, mod3,
                      ln1_g[l].reshape(1, -1), ln1_b[l].reshape(1, -1))
    return _ffn(h2, x1, w_gate[l].astype(BF16), w_up[l].astype(BF16), w_down[l].astype(BF16),
                mod3, ln2_g[l].reshape(1, -1), ln2_b[l].reshape(1, -1))
```

```python
import functools
import math

import jax
import jax.numpy as jnp
from jax import lax
from jax.experimental import pallas as pl
from jax.experimental.pallas import tpu as pltpu

D_MODEL = 1024
GRID_W = 64
HEAD_DIM = 64
NA_HEADS = 8
NA_WIDTH = NA_HEADS * HEAD_DIM
NA_KR = 8
NA_KC = 16
DA_HEADS = 4
DA_VDIM = 2 * HEAD_DIM
DA_QK = DA_HEADS * 2 * HEAD_DIM
DA_WIDTH = DA_HEADS * DA_VDIM
IN_COLS = 3 * NA_WIDTH + 2 * DA_QK + DA_WIDTH
ROPE_THETA = 10000.0
LN_EPS = 1e-5
DEPTH = 1
ALPHA = (2.0 * DEPTH) ** 0.25
LAMBDA_INIT = 0.8 - 0.6 * math.exp(-0.3 * 0)
LOG2E = math.log2(math.e)
Q_SCALE = HEAD_DIM ** -0.5 * LOG2E

LANES = 128
SUBLANES = 8
V7X_VMEM_BYTES = 64 * 1024 * 1024
NEG_BIG = -1e30

BF16 = jnp.bfloat16
F32 = jnp.float32

TM = 512
NA_RB = 8
TQ = 1024
TK = 2048
DA_SUB = 1024
TK_EXACT = 512
DA_EST = 128
ADA_TN = 1536
VMEM_LIMIT = 56 * 1024 * 1024


def _cparams(sem):
    return pltpu.CompilerParams(dimension_semantics=sem, vmem_limit_bytes=VMEM_LIMIT)


def _ln(x):
    mu = jnp.mean(x, axis=-1, keepdims=True)
    xc = x - mu
    var = jnp.mean(xc * xc, axis=-1, keepdims=True)
    return xc * lax.rsqrt(var + LN_EPS)


def _silu(x):
    return x / (1.0 + jnp.exp(-x))


def _ada_kernel(c_ref, w_ref, b_ref, o_ref):
    act = _silu(c_ref[...])
    o_ref[...] = jnp.dot(act, w_ref[...], preferred_element_type=F32) + b_ref[...]


def _ada(c_pad, w, b):
    rows, d = c_pad.shape
    n = w.shape[1]
    return pl.pallas_call(
        _ada_kernel,
        grid=(n // ADA_TN,),
        in_specs=[
            pl.BlockSpec((rows, d), lambda j: (0, 0)),
            pl.BlockSpec((d, ADA_TN), lambda j: (0, j)),
            pl.BlockSpec((1, ADA_TN), lambda j: (0, j)),
        ],
        out_specs=pl.BlockSpec((rows, ADA_TN), lambda j: (0, j)),
        out_shape=jax.ShapeDtypeStruct((rows, n), F32),
        compiler_params=_cparams(("arbitrary",)),
        name="ada",
    )(c_pad, w, b)


def _inv_freq_lanes(shape):
    lane = lax.broadcasted_iota(jnp.int32, shape, len(shape) - 1)
    f = (lane & (HEAD_DIM // 2 - 1)).astype(F32)
    return jnp.exp(f * (-math.log(ROPE_THETA) / (HEAD_DIM // 2)))


def _rope_tab_kernel(cos_ref, sin_ref):
    r = lax.broadcasted_iota(jnp.int32, (TM, LANES), 0).astype(F32)
    ang = r * _inv_freq_lanes((TM, LANES))
    cos_ref[...] = jnp.cos(ang)
    sin_ref[...] = jnp.sin(ang)


def _rope_tab():
    return pl.pallas_call(
        _rope_tab_kernel,
        out_shape=(jax.ShapeDtypeStruct((TM, LANES), F32),) * 2,
        name="rope_tab",
    )()


def _rpb_tab_kernel(rpb_ref, o_ref):
    rr = pl.program_id(0)
    c = lax.broadcasted_iota(jnp.int32, (GRID_W, LANES), 0)
    lane = lax.broadcasted_iota(jnp.int32, (GRID_W, LANES), 1)
    w = lane & (GRID_W - 1)
    first_head = lane < GRID_W
    t = c - w + (NA_KC - 1)
    cs = jnp.clip(w - NA_KC // 2, 0, GRID_W - NA_KC)
    off = c - cs
    n_rel = 2 * NA_KC - 1
    n_row = 2 * NA_KR - 1
    for p in range(NA_HEADS // 2):
        acc = jnp.zeros((GRID_W, LANES), F32)
        for tt in range(n_rel):
            va = rpb_ref[((2 * p) * n_row + rr) * n_rel + tt] * LOG2E
            vb = rpb_ref[((2 * p + 1) * n_row + rr) * n_rel + tt] * LOG2E
            acc = jnp.where(t == tt, jnp.where(first_head, va, vb), acc)
        acc = jnp.where(off >= 0, acc, NEG_BIG)
        o_ref[p] = jnp.where(off < NA_KC, acc, NEG_BIG)


def _rpb_tab(rpb_flat):
    n_row = 2 * NA_KR - 1
    return pl.pallas_call(
        _rpb_tab_kernel,
        grid=(n_row,),
        in_specs=[pl.BlockSpec(memory_space=pltpu.SMEM)],
        out_specs=pl.BlockSpec((None, NA_HEADS // 2, GRID_W, LANES), lambda r: (r, 0, 0, 0)),
        out_shape=jax.ShapeDtypeStruct((n_row, NA_HEADS // 2, GRID_W, LANES), F32),
        compiler_params=_cparams(("arbitrary",)),
        name="rpb_tab",
    )(rpb_flat)


def _rope(x, cosv, sin_signed, first_half):
    swapped = jnp.where(first_half, pltpu.roll(x, LANES - HEAD_DIM // 2, 1),
                        pltpu.roll(x, HEAD_DIM // 2, 1))
    return x * cosv + swapped * sin_signed


def _inproj_kernel(x_ref, mod_ref, w_ref, cosb_ref, sinb_ref,
                   naq_ref, nak_ref, nav_ref, daq_ref, dak_ref, davt_ref):
    i = pl.program_id(1)
    shift1 = mod_ref[:, 0:D_MODEL]
    scale1 = mod_ref[:, D_MODEL:2 * D_MODEL]
    h = _ln(x_ref[...]) * (1.0 + scale1) + shift1
    proj = jnp.dot(h.astype(BF16), w_ref[...], preferred_element_type=F32)

    o1, o2, o3 = NA_WIDTH, 2 * NA_WIDTH, 3 * NA_WIDTH
    o4, o5 = o3 + DA_QK, o3 + 2 * DA_QK
    naq_ref[...] = (proj[:, :o1] * Q_SCALE).astype(BF16)
    nak_ref[...] = proj[:, o1:o2].astype(BF16)
    nav_ref[...] = proj[:, o2:o3].astype(BF16)

    base = (i * TM).astype(F32) * _inv_freq_lanes((1, LANES))
    cosa, sina = jnp.cos(base), jnp.sin(base)
    cosb, sinb = cosb_ref[...], sinb_ref[...]
    cosv = cosa * cosb - sina * sinb
    sinv = sina * cosb + cosa * sinb
    lane = lax.broadcasted_iota(jnp.int32, (TM, LANES), 1)
    first_half = (lane & (HEAD_DIM - 1)) < HEAD_DIM // 2
    sin_signed = jnp.where(first_half, -sinv, sinv)
    for g in range(DA_QK // LANES):
        sl = slice(g * LANES, (g + 1) * LANES)
        q = _rope(proj[:, o3 + g * LANES:o3 + (g + 1) * LANES], cosv, sin_signed, first_half)
        daq_ref[:, sl] = (q * Q_SCALE).astype(BF16)
        k = _rope(proj[:, o4 + g * LANES:o4 + (g + 1) * LANES], cosv, sin_signed, first_half)
        dak_ref[:, sl] = k.astype(BF16)
    davt_ref[...] = proj[:, o5:].T.astype(BF16)


def _inproj(x, mod3, w_in, cosb, sinb):
    b, s, d = x.shape
    tok = lambda width: pl.BlockSpec((None, TM, width), lambda bi, i: (bi, i, 0))
    sds = lambda width: jax.ShapeDtypeStruct((b, s, width), BF16)
    return pl.pallas_call(
        _inproj_kernel,
        grid=(b, s // TM),
        in_specs=[
            tok(d),
            pl.BlockSpec((None, 1, mod3.shape[-1]), lambda bi, i: (bi, 0, 0)),
            pl.BlockSpec((d, IN_COLS), lambda bi, i: (0, 0)),
            pl.BlockSpec((TM, LANES), lambda bi, i: (0, 0)),
            pl.BlockSpec((TM, LANES), lambda bi, i: (0, 0)),
        ],
        out_specs=[tok(NA_WIDTH), tok(NA_WIDTH), tok(NA_WIDTH), tok(DA_QK), tok(DA_QK),
                   pl.BlockSpec((None, DA_WIDTH, TM), lambda bi, i: (bi, 0, i))],
        out_shape=[sds(NA_WIDTH), sds(NA_WIDTH), sds(NA_WIDTH), sds(DA_QK), sds(DA_QK),
                   jax.ShapeDtypeStruct((b, DA_WIDTH, s), BF16)],
        compiler_params=_cparams(("arbitrary", "arbitrary")),
        name="inproj",
    )(x, mod3, w_in, cosb, sinb)


def _nattn_kernel(q_ref, k_ref, v_ref, t_ref, o_ref, *, rows):
    n_keys = NA_KR * GRID_W
    lane = lax.broadcasted_iota(jnp.int32, (GRID_W, LANES), 1)
    first_head = lane < GRID_W
    for i in range(NA_RB):
        r = pl.program_id(1) * NA_RB + i
        rs = jnp.clip(r - NA_KR // 2, 0, rows - NA_KR)
        d = r - rs
        start = pl.multiple_of(rs * GRID_W, GRID_W)
        tok = slice(i * GRID_W, (i + 1) * GRID_W)
        for p in range(NA_HEADS // 2):
            sl = slice(p * LANES, (p + 1) * LANES)
            q2 = q_ref[tok, sl]
            zero = jnp.zeros_like(q2)
            rhs = jnp.concatenate([jnp.where(first_head, q2, zero),
                                   jnp.where(first_head, zero, q2)], axis=0)
            k2 = k_ref[pl.ds(start, n_keys), sl]
            v2 = v_ref[pl.ds(start, n_keys), sl]
            st = lax.dot_general(k2, rhs, (((1,), (1,)), ((), ())),
                                 preferred_element_type=F32)
            st = st + t_ref[pl.ds(NA_KR - 1 - d, NA_KR), p].reshape(n_keys, LANES)
            m = jnp.max(st, axis=0, keepdims=True)
            e = jnp.exp2(st - m)
            l = jnp.sum(e, axis=0, keepdims=True)
            pn = (e * (1.0 / l)).T.astype(BF16)
            o2 = jnp.dot(pn, v2, preferred_element_type=F32)
            o_ref[tok, sl] = jnp.where(first_head, o2[:GRID_W], o2[GRID_W:]).astype(o_ref.dtype)


def _nattn(q, k, v, tab):
    b, s, width = q.shape
    rows = s // GRID_W
    full = pl.BlockSpec((None, s, width), lambda bi, r: (bi, 0, 0))
    row = pl.BlockSpec((None, NA_RB * GRID_W, width), lambda bi, r: (bi, r, 0))
    return pl.pallas_call(
        functools.partial(_nattn_kernel, rows=rows),
        grid=(b, rows // NA_RB),
        in_specs=[row, full, full,
                  pl.BlockSpec(tab.shape, lambda bi, r: (0, 0, 0, 0))],
        out_specs=row,
        out_shape=jax.ShapeDtypeStruct((b, s, width), BF16),
        compiler_params=_cparams(("arbitrary", "arbitrary")),
        name="nattn",
    )(q, k, v, tab)


def _da_scores(k_blk, rhs):
    return lax.dot_general(k_blk, rhs, (((1,), (1,)), ((), ())), preferred_element_type=F32)


def _da_output(acc1, acc2, l1, l2, lam, g):
    o = acc1 * (1.0 / l1) - lam * (acc2 * (1.0 / l2))
    ms = jnp.mean(o * o, axis=0, keepdims=True)
    o = o * lax.rsqrt(ms + LN_EPS)
    return o.T * g * (1.0 - LAMBDA_INIT)


def _da_stream(rhs, k_ref, vt_ref, acc_ref, n_keys):
    k0 = k_ref[0:DA_EST, :]
    ref8 = [jnp.broadcast_to(jnp.max(_da_scores(k0, r), axis=0, keepdims=True), (SUBLANES, TQ))
            for r in rhs]
    acc_ref[...] = jnp.zeros_like(acc_ref)

    def body(j, carry):
        out = []
        for mp in range(2):
            l8 = carry[mp]
            pv = None
            for c in range(TK // DA_SUB):
                start = pl.multiple_of(j * TK + c * DA_SUB, DA_SUB)
                s = _da_scores(k_ref[pl.ds(start, DA_SUB), :], rhs[mp])
                p = jnp.exp2(s.reshape(DA_SUB // SUBLANES, SUBLANES, TQ) - ref8[mp])
                l8 = l8 + jnp.sum(p, axis=0)
                d = jnp.dot(vt_ref[:, pl.ds(start, DA_SUB)],
                            p.reshape(DA_SUB, TQ).astype(BF16), preferred_element_type=F32)
                pv = d if pv is None else pv + d
            acc_ref[mp] += pv
            out.append(l8)
        return tuple(out)

    z8 = jnp.zeros((SUBLANES, TQ), F32)
    l8 = lax.fori_loop(0, n_keys // TK, body, (z8, z8))
    return [jnp.sum(l, axis=0, keepdims=True) for l in l8]


def _da_online(rhs, k_ref, vt_ref, acc_ref, n_keys):
    acc_ref[...] = jnp.zeros_like(acc_ref)

    def body(j, carry):
        start = pl.multiple_of(j * TK_EXACT, TK_EXACT)
        kb = k_ref[pl.ds(start, TK_EXACT), :]
        vb = vt_ref[:, pl.ds(start, TK_EXACT)]
        new = []
        for mp in range(2):
            m_old, l_old = carry[2 * mp], carry[2 * mp + 1]
            s = _da_scores(kb, rhs[mp])
            m_new = jnp.maximum(m_old, jnp.max(s, axis=0, keepdims=True))
            a = jnp.exp2(m_old - m_new)
            p = jnp.exp2(s - m_new)
            l_new = a * l_old + jnp.sum(p, axis=0, keepdims=True)
            acc_ref[mp] = acc_ref[mp] * a + jnp.dot(vb, p.astype(BF16),
                                                    preferred_element_type=F32)
            new += [m_new, l_new]
        return tuple(new)

    m0 = jnp.full((1, TQ), -jnp.inf, F32)
    l0 = jnp.zeros((1, TQ), F32)
    _, l1, _, l2 = lax.fori_loop(0, n_keys // TK_EXACT, body, (m0, l0, m0, l0))
    return [l1, l2]


def _dattn_kernel(q_ref, k_ref, vt_ref, lam_ref, g_ref, o_ref, acc_ref, *, n_keys):
    q = q_ref[...]
    lane = lax.broadcasted_iota(jnp.int32, q.shape, 1)
    zero = jnp.zeros_like(q)
    rhs = (jnp.where(lane < HEAD_DIM, q, zero), jnp.where(lane < HEAD_DIM, zero, q))
    lam = (jnp.exp(jnp.sum(lam_ref[0:1, :] * lam_ref[1:2, :], axis=-1, keepdims=True))
           - jnp.exp(jnp.sum(lam_ref[2:3, :] * lam_ref[3:4, :], axis=-1, keepdims=True))
           + LAMBDA_INIT)

    l1, l2 = _da_stream(rhs, k_ref, vt_ref, acc_ref, n_keys)
    out = _da_output(acc_ref[0], acc_ref[1], l1, l2, lam, g_ref[...])
    o_ref[...] = out.astype(o_ref.dtype)
    overflowed = (jnp.sum(jnp.where(jnp.isfinite(out), 0.0, 1.0))
                  + jnp.sum(jnp.where(jnp.isfinite(l1 + l2), 0.0, 1.0)))

    @pl.when(overflowed > 0.0)
    def _():
        e1, e2 = _da_online(rhs, k_ref, vt_ref, acc_ref, n_keys)
        o_ref[...] = _da_output(acc_ref[0], acc_ref[1], e1, e2, lam,
                                g_ref[...]).astype(o_ref.dtype)


def _dattn(q, k, vt, lam4, g):
    b, s, _ = q.shape
    return pl.pallas_call(
        functools.partial(_dattn_kernel, n_keys=s),
        grid=(b, DA_HEADS, s // TQ),
        in_specs=[
            pl.BlockSpec((None, TQ, LANES), lambda bi, h, i: (bi, i, h)),
            pl.BlockSpec((None, s, LANES), lambda bi, h, i: (bi, 0, h)),
            pl.BlockSpec((None, DA_VDIM, s), lambda bi, h, i: (bi, h, 0)),
            pl.BlockSpec(lam4.shape, lambda bi, h, i: (0, 0)),
            pl.BlockSpec(g.shape, lambda bi, h, i: (0, 0)),
        ],
        out_specs=pl.BlockSpec((None, TQ, DA_VDIM), lambda bi, h, i: (bi, i, h)),
        out_shape=jax.ShapeDtypeStruct((b, s, DA_WIDTH), BF16),
        scratch_shapes=[pltpu.VMEM((2, DA_VDIM, TQ), F32)],
        compiler_params=_cparams(("arbitrary", "arbitrary", "arbitrary")),
        name="dattn",
    )(q, k, vt, lam4, g)


def _outproj_kernel(oa_ref, ob_ref, w_ref, x_ref, mod_ref, g_ref, b_ref, x1_ref, h2_ref):
    mix = (jnp.dot(oa_ref[...], w_ref[0:NA_WIDTH, :], preferred_element_type=F32)
           + jnp.dot(ob_ref[...], w_ref[NA_WIDTH:, :], preferred_element_type=F32))
    gate1 = mod_ref[:, 2 * D_MODEL:3 * D_MODEL]
    shift2 = mod_ref[:, 3 * D_MODEL:4 * D_MODEL]
    scale2 = mod_ref[:, 4 * D_MODEL:5 * D_MODEL]
    x1 = _ln(ALPHA * x_ref[...] + (1.0 + gate1) * mix) * g_ref[...] + b_ref[...]
    x1_ref[...] = x1
    h2_ref[...] = (_ln(x1) * (1.0 + scale2) + shift2).astype(h2_ref.dtype)


def _outproj(oa, ob, w_out, x, mod3, g, bb):
    b, s, d = x.shape
    tok = lambda width: pl.BlockSpec((None, TM, width), lambda bi, i: (bi, i, 0))
    vec = pl.BlockSpec((1, d), lambda bi, i: (0, 0))
    return pl.pallas_call(
        _outproj_kernel,
        grid=(b, s // TM),
        in_specs=[tok(NA_WIDTH), tok(DA_WIDTH),
                  pl.BlockSpec(w_out.shape, lambda bi, i: (0, 0)),
                  tok(d),
                  pl.BlockSpec((None, 1, mod3.shape[-1]), lambda bi, i: (bi, 0, 0)),
                  vec, vec],
        out_specs=[tok(d), tok(d)],
        out_shape=[jax.ShapeDtypeStruct((b, s, d), F32), jax.ShapeDtypeStruct((b, s, d), BF16)],
        compiler_params=_cparams(("arbitrary", "arbitrary")),
        name="outproj",
    )(oa, ob, w_out, x, mod3, g, bb)


def _ffn_kernel(h_ref, x1_ref, wg_ref, wu_ref, wd_ref, mod_ref, g_ref, b_ref, o_ref):
    h = h_ref[...]
    gate = jnp.dot(h, wg_ref[...], preferred_element_type=F32)
    up = jnp.dot(h, wu_ref[...], preferred_element_type=F32)
    act = (_silu(gate) * up).astype(BF16)
    ffn = jnp.dot(act, wd_ref[...], preferred_element_type=F32)
    gate2 = mod_ref[:, 5 * D_MODEL:6 * D_MODEL]
    y = ALPHA * x1_ref[...] + (1.0 + gate2) * ffn
    o_ref[...] = _ln(y) * g_ref[...] + b_ref[...]


def _ffn(h2, x1, wg, wu, wd, mod3, g, bb):
    b, s, d = x1.shape
    tok = pl.BlockSpec((None, TM, d), lambda bi, i: (bi, i, 0))
    vec = pl.BlockSpec((1, d), lambda bi, i: (0, 0))
    once = lambda shape: pl.BlockSpec(shape, lambda bi, i: (0, 0), pipeline_mode=pl.Buffered(1))
    return pl.pallas_call(
        _ffn_kernel,
        grid=(b, s // TM),
        in_specs=[tok, tok, once(wg.shape), once(wu.shape), once(wd.shape),
                  pl.BlockSpec((None, 1, mod3.shape[-1]), lambda bi, i: (bi, 0, 0)),
                  vec, vec],
        out_specs=tok,
        out_shape=jax.ShapeDtypeStruct((b, s, d), F32),
        compiler_params=_cparams(("arbitrary", "arbitrary")),
        name="ffn",
    )(h2, x1, wg, wu, wd, mod3, g, bb)


def kernel(x, c, w_ada, b_ada, w_in, rpb, lambda_q1, lambda_k1, lambda_q2, lambda_k2,
           subln_g, w_out, ln1_g, ln1_b, w_gate, w_up, w_down, ln2_g, ln2_b):
    b, s, d = x.shape
    assert d == D_MODEL and s % TM == 0 and s % TQ == 0 and s % TK == 0 and s % GRID_W == 0
    assert s % TK_EXACT == 0 and TK % DA_SUB == 0
    assert w_ada.shape[0] == DEPTH == 1
    l = 0
    sub = 8
    c_pad = jnp.zeros((sub, d), F32).at[:b].set(c)
    mod = _ada(c_pad, w_ada[l], b_ada[l].reshape(1, -1))[:b]
    mod3 = mod.reshape(b, 1, -1)

    cosb, sinb = _rope_tab()
    tab = _rpb_tab(rpb[l].reshape(-1))

    naq, nak, nav, daq, dak, davt = _inproj(x, mod3, w_in[l].astype(BF16), cosb, sinb)
    out_a = _nattn(naq, nak, nav, tab)
    lam4 = jnp.stack([lambda_q1[l], lambda_k1[l], lambda_q2[l], lambda_k2[l]])
    out_b = _dattn(daq, dak, davt, lam4, subln_g[l].reshape(1, -1))
    x1, h2 = _outproj(out_a, out_b, w_out[l].astype(BF16), x, mod3,
                      ln1_g[l].reshape(1, -1), ln1_b[l].reshape(1, -1))
    return _ffn(h2, x1, w_gate[l].astype(BF16), w_up[l].astype(BF16), w_down[l].astype(BF16),
                mod3, ln2_g[l].reshape(1, -1), ln2_b[l].reshape(1, -1))
```

```python
import functools
import math

import jax
import jax.numpy as jnp
from jax import lax
from jax.experimental import pallas as pl
from jax.experimental.pallas import tpu as pltpu

D_MODEL = 1024
GRID_W = 64
HEAD_DIM = 64
NA_HEADS = 8
NA_WIDTH = NA_HEADS * HEAD_DIM
NA_KR = 8
NA_KC = 16
DA_HEADS = 4
DA_VDIM = 2 * HEAD_DIM
DA_QK = DA_HEADS * 2 * HEAD_DIM
DA_WIDTH = DA_HEADS * DA_VDIM
IN_COLS = 3 * NA_WIDTH + 2 * DA_QK + DA_WIDTH
ROPE_THETA = 10000.0
LN_EPS = 1e-5
DEPTH = 1
ALPHA = (2.0 * DEPTH) ** 0.25
LAMBDA_INIT = 0.8 - 0.6 * math.exp(-0.3 * 0)
LOG2E = math.log2(math.e)
Q_SCALE = HEAD_DIM ** -0.5 * LOG2E

LANES = 128
SUBLANES = 8
V7X_VMEM_BYTES = 64 * 1024 * 1024
NEG_BIG = -1e30

BF16 = jnp.bfloat16
F32 = jnp.float32

TM = 1024
TM_SUB = 512
TM_MLP = 1024
MLP_SUB = 256
FF_SPLITS = (0, 1536, 2816)
NA_RB = 8
TQ = 2048
TK = 8192
DA_SUB = 2048
TK_EXACT = 512
DA_EST = 128
ADA_TN = 1536
VMEM_LIMIT = 56 * 1024 * 1024


def _cparams(sem):
    return pltpu.CompilerParams(dimension_semantics=sem, vmem_limit_bytes=VMEM_LIMIT)


def _ln(x):
    mu = jnp.mean(x, axis=-1, keepdims=True)
    xc = x - mu
    var = jnp.mean(xc * xc, axis=-1, keepdims=True)
    return xc * lax.rsqrt(var + LN_EPS)


def _silu(x):
    return x / (1.0 + jnp.exp(-x))


def _ada_kernel(c_ref, w_ref, b_ref, o_ref):
    act = _silu(c_ref[...])
    o_ref[...] = jnp.dot(act, w_ref[...], preferred_element_type=F32) + b_ref[...]


def _ada(c_pad, w, b):
    rows, d = c_pad.shape
    n = w.shape[1]
    return pl.pallas_call(
        _ada_kernel,
        grid=(n // ADA_TN,),
        in_specs=[
            pl.BlockSpec((rows, d), lambda j: (0, 0)),
            pl.BlockSpec((d, ADA_TN), lambda j: (0, j)),
            pl.BlockSpec((1, ADA_TN), lambda j: (0, j)),
        ],
        out_specs=pl.BlockSpec((rows, ADA_TN), lambda j: (0, j)),
        out_shape=jax.ShapeDtypeStruct((rows, n), F32),
        compiler_params=_cparams(("arbitrary",)),
        name="ada",
    )(c_pad, w, b)


def _inv_freq_lanes(shape):
    lane = lax.broadcasted_iota(jnp.int32, shape, len(shape) - 1)
    f = (lane & (HEAD_DIM // 2 - 1)).astype(F32)
    return jnp.exp(f * (-math.log(ROPE_THETA) / (HEAD_DIM // 2)))


def _rope_tab_kernel(cos_ref, sin_ref):
    r = lax.broadcasted_iota(jnp.int32, (TM, LANES), 0).astype(F32)
    ang = r * _inv_freq_lanes((TM, LANES))
    cos_ref[...] = jnp.cos(ang)
    sin_ref[...] = jnp.sin(ang)


def _rope_tab():
    return pl.pallas_call(
        _rope_tab_kernel,
        out_shape=(jax.ShapeDtypeStruct((TM, LANES), F32),) * 2,
        name="rope_tab",
    )()


def _rpb_tab_kernel(rpb_ref, o_ref):
    p = pl.program_id(0)
    row = lax.broadcasted_iota(jnp.int32, (2 * GRID_W, LANES), 0)
    lane = lax.broadcasted_iota(jnp.int32, (2 * GRID_W, LANES), 1)
    w = row & (GRID_W - 1)
    c = lane & (GRID_W - 1)
    second_head = row >= GRID_W
    t = c - w + (NA_KC - 1)
    off = c - jnp.clip(w - NA_KC // 2, 0, GRID_W - NA_KC)
    n_rel = 2 * NA_KC - 1
    n_row = 2 * NA_KR - 1
    base = []
    for rr in range(n_row):
        acc = jnp.zeros((2 * GRID_W, LANES), F32)
        for tt in range(n_rel):
            va = rpb_ref[((2 * p) * n_row + rr) * n_rel + tt] * LOG2E
            vb = rpb_ref[((2 * p + 1) * n_row + rr) * n_rel + tt] * LOG2E
            acc = jnp.where(t == tt, jnp.where(second_head, vb, va), acc)
        acc = jnp.where(off >= 0, acc, NEG_BIG)
        base.append(jnp.where(off < NA_KC, acc, NEG_BIG))
    even_key_row = lane < GRID_W
    for d in range(NA_KR):
        for a in range(NA_KR // 2):
            lo = 2 * a - d + NA_KR - 1
            o_ref[d, :, a * LANES:(a + 1) * LANES] = jnp.where(even_key_row, base[lo], base[lo + 1])


def _rpb_tab(rpb_flat):
    pairs = NA_HEADS // 2
    shape = (NA_KR, pairs, 2 * GRID_W, NA_KR * GRID_W)
    return pl.pallas_call(
        _rpb_tab_kernel,
        grid=(pairs,),
        in_specs=[pl.BlockSpec(memory_space=pltpu.SMEM)],
        out_specs=pl.BlockSpec((NA_KR, None) + shape[2:], lambda p: (0, p, 0, 0)),
        out_shape=jax.ShapeDtypeStruct(shape, F32),
        compiler_params=_cparams(("arbitrary",)),
        name="rpb_tab",
    )(rpb_flat)


def _rope(x, cosv, sin_signed, first_half):
    swapped = jnp.where(first_half, pltpu.roll(x, LANES - HEAD_DIM // 2, 1),
                        pltpu.roll(x, HEAD_DIM // 2, 1))
    return x * cosv + swapped * sin_signed


def _inproj_kernel(x_ref, mod_ref, w_ref, cosb_ref, sinb_ref,
                   naq_ref, nak_ref, nav_ref, daq_ref, dak_ref, davt_ref):
    i = pl.program_id(1)
    shift1 = mod_ref[:, 0:D_MODEL]
    scale1 = mod_ref[:, D_MODEL:2 * D_MODEL]
    o1, o2, o3 = NA_WIDTH, 2 * NA_WIDTH, 3 * NA_WIDTH
    o4, o5 = o3 + DA_QK, o3 + 2 * DA_QK
    base = (i * TM).astype(F32) * _inv_freq_lanes((1, LANES))
    cosa, sina = jnp.cos(base), jnp.sin(base)
    lane = lax.broadcasted_iota(jnp.int32, (TM_SUB, LANES), 1)
    first_half = (lane & (HEAD_DIM - 1)) < HEAD_DIM // 2
    for t in range(TM // TM_SUB):
        rows = slice(t * TM_SUB, (t + 1) * TM_SUB)
        h = _ln(x_ref[rows, :]) * (1.0 + scale1) + shift1
        proj = jnp.dot(h.astype(BF16), w_ref[...], preferred_element_type=F32)
        naq_ref[rows, :] = (proj[:, :o1] * Q_SCALE).astype(BF16)
        nak_ref[rows, :] = proj[:, o1:o2].astype(BF16)
        nav_ref[rows, :] = proj[:, o2:o3].astype(BF16)
        cosb, sinb = cosb_ref[rows, :], sinb_ref[rows, :]
        cosv = cosa * cosb - sina * sinb
        sinv = sina * cosb + cosa * sinb
        sin_signed = jnp.where(first_half, -sinv, sinv)
        for g in range(DA_QK // LANES):
            sl = slice(g * LANES, (g + 1) * LANES)
            q = _rope(proj[:, o3 + g * LANES:o3 + (g + 1) * LANES], cosv, sin_signed, first_half)
            daq_ref[rows, sl] = (q * Q_SCALE).astype(BF16)
            k = _rope(proj[:, o4 + g * LANES:o4 + (g + 1) * LANES], cosv, sin_signed, first_half)
            dak_ref[rows, sl] = k.astype(BF16)
        davt_ref[:, rows] = proj[:, o5:].T.astype(BF16)


def _inproj(x, mod3, w_in, cosb, sinb):
    b, s, d = x.shape
    tok = lambda width: pl.BlockSpec((None, TM, width), lambda bi, i: (bi, i, 0))
    sds = lambda width: jax.ShapeDtypeStruct((b, s, width), BF16)
    return pl.pallas_call(
        _inproj_kernel,
        grid=(b, s // TM),
        in_specs=[
            tok(d),
            pl.BlockSpec((None, 1, mod3.shape[-1]), lambda bi, i: (bi, 0, 0)),
            pl.BlockSpec((d, IN_COLS), lambda bi, i: (0, 0)),
            pl.BlockSpec((TM, LANES), lambda bi, i: (0, 0)),
            pl.BlockSpec((TM, LANES), lambda bi, i: (0, 0)),
        ],
        out_specs=[tok(NA_WIDTH), tok(NA_WIDTH), tok(NA_WIDTH), tok(DA_QK), tok(DA_QK),
                   pl.BlockSpec((None, DA_WIDTH, TM), lambda bi, i: (bi, 0, i))],
        out_shape=[sds(NA_WIDTH), sds(NA_WIDTH), sds(NA_WIDTH), sds(DA_QK), sds(DA_QK),
                   jax.ShapeDtypeStruct((b, DA_WIDTH, s), BF16)],
        compiler_params=_cparams(("arbitrary", "arbitrary")),
        name="inproj",
    )(x, mod3, w_in, cosb, sinb)


def _nattn_kernel(q_ref, k_ref, v_ref, t_ref, o_ref, *, rows):
    n_keys = NA_KR * GRID_W
    lane = lax.broadcasted_iota(jnp.int32, (GRID_W, LANES), 1)
    first_head = lane < GRID_W
    ones = jnp.ones((n_keys, LANES), BF16)
    for i in range(NA_RB):
        r = pl.program_id(1) * NA_RB + i
        rs = jnp.clip(r - NA_KR // 2, 0, rows - NA_KR)
        d = r - rs
        start = pl.multiple_of(rs * GRID_W, GRID_W)
        tok = slice(i * GRID_W, (i + 1) * GRID_W)
        for p in range(NA_HEADS // 2):
            sl = slice(p * LANES, (p + 1) * LANES)
            q2 = q_ref[tok, sl]
            zero = jnp.zeros_like(q2)
            lhs = jnp.concatenate([jnp.where(first_head, q2, zero),
                                   jnp.where(first_head, zero, q2)], axis=0)
            k2 = k_ref[pl.ds(start, n_keys), sl]
            v2 = v_ref[pl.ds(start, n_keys), sl]
            s = lax.dot_general(lhs, k2, (((1,), (1,)), ((), ())),
                                preferred_element_type=F32)
            s = s + t_ref[d, p]
            e = jnp.exp2(s - jnp.max(s, axis=-1, keepdims=True)).astype(BF16)
            o2 = jnp.dot(e, jnp.concatenate([v2, ones], axis=1),
                         preferred_element_type=F32)
            o2 = o2[:, :LANES] * (1.0 / o2[:, LANES:])
            o_ref[tok, sl] = jnp.where(first_head, o2[:GRID_W], o2[GRID_W:]).astype(o_ref.dtype)


def _nattn(q, k, v, tab):
    b, s, width = q.shape
    rows = s // GRID_W
    once = pl.Buffered(1)
    full = pl.BlockSpec((None, s, width), lambda bi, r: (bi, 0, 0), pipeline_mode=once)
    row = pl.BlockSpec((None, NA_RB * GRID_W, width), lambda bi, r: (bi, r, 0))
    return pl.pallas_call(
        functools.partial(_nattn_kernel, rows=rows),
        grid=(b, rows // NA_RB),
        in_specs=[row, full, full,
                  pl.BlockSpec(tab.shape, lambda bi, r: (0, 0, 0, 0), pipeline_mode=once)],
        out_specs=row,
        out_shape=jax.ShapeDtypeStruct((b, s, width), BF16),
        compiler_params=_cparams(("arbitrary", "arbitrary")),
        name="nattn",
    )(q, k, v, tab)


def _da_scores(k_blk, rhs):
    return lax.dot_general(k_blk, rhs, (((1,), (1,)), ((), ())), preferred_element_type=F32)


def _da_output(acc1, acc2, l1, l2, lam, g):
    o = acc1 * (1.0 / l1) - lam * (acc2 * (1.0 / l2))
    ms = jnp.mean(o * o, axis=0, keepdims=True)
    o = o * lax.rsqrt(ms + LN_EPS)
    return o.T * g * (1.0 - LAMBDA_INIT)


def _da_stream(rhs, k_ref, vt_ref, acc_ref, n_keys):
    k0 = k_ref[0:DA_EST, :]
    ref8 = [jnp.broadcast_to(jnp.max(_da_scores(k0, r), axis=0, keepdims=True), (SUBLANES, TQ))
            for r in rhs]
    acc_ref[...] = jnp.zeros_like(acc_ref)

    def body(j, carry):
        out = []
        for mp in range(2):
            l8 = carry[mp]
            pv = None
            for c in range(TK // DA_SUB):
                start = pl.multiple_of(j * TK + c * DA_SUB, DA_SUB)
                s = _da_scores(k_ref[pl.ds(start, DA_SUB), :], rhs[mp])
                p = jnp.exp2(s.reshape(DA_SUB // SUBLANES, SUBLANES, TQ) - ref8[mp])
                l8 = l8 + jnp.sum(p, axis=0)
                d = jnp.dot(vt_ref[:, pl.ds(start, DA_SUB)],
                            p.reshape(DA_SUB, TQ).astype(BF16), preferred_element_type=F32)
                pv = d if pv is None else pv + d
            acc_ref[mp] += pv
            out.append(l8)
        return tuple(out)

    z8 = jnp.zeros((SUBLANES, TQ), F32)
    l8 = lax.fori_loop(0, n_keys // TK, body, (z8, z8))
    return [jnp.sum(l, axis=0, keepdims=True) for l in l8]


def _da_online(rhs, k_ref, vt_ref, acc_ref, n_keys):
    acc_ref[...] = jnp.zeros_like(acc_ref)

    def body(j, carry):
        start = pl.multiple_of(j * TK_EXACT, TK_EXACT)
        kb = k_ref[pl.ds(start, TK_EXACT), :]
        vb = vt_ref[:, pl.ds(start, TK_EXACT)]
        new = []
        for mp in range(2):
            m_old, l_old = carry[2 * mp], carry[2 * mp + 1]
            s = _da_scores(kb, rhs[mp])
            m_new = jnp.maximum(m_old, jnp.max(s, axis=0, keepdims=True))
            a = jnp.exp2(m_old - m_new)
            p = jnp.exp2(s - m_new)
            l_new = a * l_old + jnp.sum(p, axis=0, keepdims=True)
            acc_ref[mp] = acc_ref[mp] * a + jnp.dot(vb, p.astype(BF16),
                                                    preferred_element_type=F32)
            new += [m_new, l_new]
        return tuple(new)

    m0 = jnp.full((1, TQ), -jnp.inf, F32)
    l0 = jnp.zeros((1, TQ), F32)
    _, l1, _, l2 = lax.fori_loop(0, n_keys // TK_EXACT, body, (m0, l0, m0, l0))
    return [l1, l2]


def _dattn_kernel(q_ref, k_ref, vt_ref, lam_ref, g_ref, o_ref, acc_ref, *, n_keys):
    q = q_ref[...]
    lane = lax.broadcasted_iota(jnp.int32, q.shape, 1)
    zero = jnp.zeros_like(q)
    rhs = (jnp.where(lane < HEAD_DIM, q, zero), jnp.where(lane < HEAD_DIM, zero, q))
    lam = (jnp.exp(jnp.sum(lam_ref[0:1, :] * lam_ref[1:2, :], axis=-1, keepdims=True))
           - jnp.exp(jnp.sum(lam_ref[2:3, :] * lam_ref[3:4, :], axis=-1, keepdims=True))
           + LAMBDA_INIT)

    l1, l2 = _da_stream(rhs, k_ref, vt_ref, acc_ref, n_keys)
    out = _da_output(acc_ref[0], acc_ref[1], l1, l2, lam, g_ref[...])
    o_ref[...] = out.astype(o_ref.dtype)
    overflowed = (jnp.sum(jnp.where(jnp.isfinite(out), 0.0, 1.0))
                  + jnp.sum(jnp.where(jnp.isfinite(l1 + l2), 0.0, 1.0)))

    @pl.when(overflowed > 0.0)
    def _():
        e1, e2 = _da_online(rhs, k_ref, vt_ref, acc_ref, n_keys)
        o_ref[...] = _da_output(acc_ref[0], acc_ref[1], e1, e2, lam,
                                g_ref[...]).astype(o_ref.dtype)


def _dattn(q, k, vt, lam4, g):
    b, s, _ = q.shape
    return pl.pallas_call(
        functools.partial(_dattn_kernel, n_keys=s),
        grid=(b, DA_HEADS, s // TQ),
        in_specs=[
            pl.BlockSpec((None, TQ, LANES), lambda bi, h, i: (bi, i, h)),
            pl.BlockSpec((None, s, LANES), lambda bi, h, i: (bi, 0, h)),
            pl.BlockSpec((None, DA_VDIM, s), lambda bi, h, i: (bi, h, 0)),
            pl.BlockSpec(lam4.shape, lambda bi, h, i: (0, 0)),
            pl.BlockSpec(g.shape, lambda bi, h, i: (0, 0)),
        ],
        out_specs=pl.BlockSpec((None, TQ, DA_VDIM), lambda bi, h, i: (bi, i, h)),
        out_shape=jax.ShapeDtypeStruct((b, s, DA_WIDTH), BF16),
        scratch_shapes=[pltpu.VMEM((2, DA_VDIM, TQ), F32)],
        compiler_params=_cparams(("arbitrary", "arbitrary", "arbitrary")),
        name="dattn",
    )(q, k, vt, lam4, g)


def _mlp_kernel(oa_ref, ob_ref, x_ref, mod_ref, wo_ref, wg_ref, wu_ref, wd_ref,
                g1_ref, b1_ref, g2_ref, b2_ref, o_ref):
    gate1 = mod_ref[:, 2 * D_MODEL:3 * D_MODEL]
    shift2 = mod_ref[:, 3 * D_MODEL:4 * D_MODEL]
    scale2 = mod_ref[:, 4 * D_MODEL:5 * D_MODEL]
    gate2 = mod_ref[:, 5 * D_MODEL:6 * D_MODEL]
    n_sub = TM_MLP // MLP_SUB
    rows = [slice(t * MLP_SUB, (t + 1) * MLP_SUB) for t in range(n_sub)]

    def attn_out(r):
        return (jnp.dot(oa_ref[r, :], wo_ref[0:NA_WIDTH, :], preferred_element_type=F32)
                + jnp.dot(ob_ref[r, :], wo_ref[NA_WIDTH:, :], preferred_element_type=F32))

    def norm1(r, mix):
        x1 = _ln(ALPHA * x_ref[r, :] + (1.0 + gate1) * mix) * g1_ref[...] + b1_ref[...]
        return x1, (_ln(x1) * (1.0 + scale2) + shift2).astype(BF16)

    def swiglu(h2):
        ffn = None
        for c0, c1 in zip(FF_SPLITS[:-1], FF_SPLITS[1:]):
            gate = jnp.dot(h2, wg_ref[:, c0:c1], preferred_element_type=F32)
            up = jnp.dot(h2, wu_ref[:, c0:c1], preferred_element_type=F32)
            act = (_silu(gate) * up).astype(BF16)
            part = jnp.dot(act, wd_ref[c0:c1, :], preferred_element_type=F32)
            ffn = part if ffn is None else ffn + part
        return ffn

    def norm2(r, x1, ffn):
        o_ref[r, :] = _ln(ALPHA * x1 + (1.0 + gate2) * ffn) * g2_ref[...] + b2_ref[...]

    mix = [attn_out(r) for r in rows]
    x1h2 = [None] * n_sub
    ffn = [None] * n_sub
    x1h2[0] = norm1(rows[0], mix[0])
    for t in range(n_sub):
        ffn[t] = swiglu(x1h2[t][1])
        if t + 1 < n_sub:
            x1h2[t + 1] = norm1(rows[t + 1], mix[t + 1])
        if t > 0:
            norm2(rows[t - 1], x1h2[t - 1][0], ffn[t - 1])
    norm2(rows[-1], x1h2[-1][0], ffn[-1])


def _mlp(oa, ob, x, mod3, w_out, wg, wu, wd, g1, b1, g2, b2):
    b, s, d = x.shape
    assert FF_SPLITS[0] == 0 and FF_SPLITS[-1] == wg.shape[1]
    tok = lambda width: pl.BlockSpec((None, TM_MLP, width), lambda bi, i: (bi, i, 0))
    vec = pl.BlockSpec((1, d), lambda bi, i: (0, 0))
    once = lambda shape: pl.BlockSpec(shape, lambda bi, i: (0, 0), pipeline_mode=pl.Buffered(1))
    return pl.pallas_call(
        _mlp_kernel,
        grid=(b, s // TM_MLP),
        in_specs=[tok(NA_WIDTH), tok(DA_WIDTH), tok(d),
                  pl.BlockSpec((None, 1, mod3.shape[-1]), lambda bi, i: (bi, 0, 0)),
                  once(w_out.shape), once(wg.shape), once(wu.shape), once(wd.shape),
                  vec, vec, vec, vec],
        out_specs=tok(d),
        out_shape=jax.ShapeDtypeStruct((b, s, d), F32),
        compiler_params=_cparams(("arbitrary", "arbitrary")),
        name="mlp",
    )(oa, ob, x, mod3, w_out, wg, wu, wd, g1, b1, g2, b2)


def kernel(x, c, w_ada, b_ada, w_in, rpb, lambda_q1, lambda_k1, lambda_q2, lambda_k2,
           subln_g, w_out, ln1_g, ln1_b, w_gate, w_up, w_down, ln2_g, ln2_b):
    b, s, d = x.shape
    assert d == D_MODEL and s % TM == 0 and s % TM_MLP == 0 and s % GRID_W == 0
    assert s % TQ == 0 and s % TK == 0 and TM % TM_SUB == 0 and TM_MLP % MLP_SUB == 0
    assert s % TK_EXACT == 0 and TK % DA_SUB == 0
    assert w_ada.shape[0] == DEPTH == 1
    l = 0
    sub = 8
    c_pad = jnp.zeros((sub, d), F32).at[:b].set(c)
    mod = _ada(c_pad, w_ada[l], b_ada[l].reshape(1, -1))[:b]
    mod3 = mod.reshape(b, 1, -1)

    cosb, sinb = _rope_tab()
    tab = _rpb_tab(rpb[l].reshape(-1))

    naq, nak, nav, daq, dak, davt = _inproj(x, mod3, w_in[l].astype(BF16), cosb, sinb)
    out_a = _nattn(naq, nak, nav, tab)
    lam4 = jnp.stack([lambda_q1[l], lambda_k1[l], lambda_q2[l], lambda_k2[l]])
    out_b = _dattn(daq, dak, davt, lam4, subln_g[l].reshape(1, -1))
    return _mlp(out_a, out_b, x, mod3, w_out[l].astype(BF16), w_gate[l].astype(BF16),
                w_up[l].astype(BF16), w_down[l].astype(BF16),
                ln1_g[l].reshape(1, -1), ln1_b[l].reshape(1, -1),
                ln2_g[l].reshape(1, -1), ln2_b[l].reshape(1, -1))
```

```python
import functools
import math

import jax
import jax.numpy as jnp
from jax import lax
from jax.experimental import pallas as pl
from jax.experimental.pallas import tpu as pltpu

D_MODEL = 1024
GRID_W = 64
HEAD_DIM = 64
NA_HEADS = 8
NA_WIDTH = NA_HEADS * HEAD_DIM
NA_KR = 8
NA_KC = 16
DA_HEADS = 4
DA_VDIM = 2 * HEAD_DIM
DA_QK = DA_HEADS * 2 * HEAD_DIM
DA_WIDTH = DA_HEADS * DA_VDIM
IN_COLS = 3 * NA_WIDTH + 2 * DA_QK + DA_WIDTH
ROPE_THETA = 10000.0
LN_EPS = 1e-5
DEPTH = 1
ALPHA = (2.0 * DEPTH) ** 0.25
LAMBDA_INIT = 0.8 - 0.6 * math.exp(-0.3 * 0)
LOG2E = math.log2(math.e)
Q_SCALE = HEAD_DIM ** -0.5 * LOG2E

LANES = 128
SUBLANES = 8
BF16_SUBLANES = 16
V7X_VMEM_BYTES = 64 * 1024 * 1024
NEG_BIG = -1e30

BF16 = jnp.bfloat16
F32 = jnp.float32

TM = 1024
TM_SUB = 512
TM_MLP = 1024
MLP_SUB = 256
FF_SPLITS = (0, 1536, 2816)
NA_RB = 8
TQ = 2048
TK = 8192
DA_SUB = 2048
TK_EXACT = 512
DA_EST = 128
ADA_TN = 1536
VMEM_LIMIT = 56 * 1024 * 1024


def _cparams(sem):
    return pltpu.CompilerParams(dimension_semantics=sem, vmem_limit_bytes=VMEM_LIMIT)


def _ln(x):
    mu = jnp.mean(x, axis=-1, keepdims=True)
    xc = x - mu
    var = jnp.mean(xc * xc, axis=-1, keepdims=True)
    return xc * lax.rsqrt(var + LN_EPS)


def _silu(x):
    return x / (1.0 + jnp.exp(-x))


def _ada_kernel(c_ref, w_ref, b_ref, o_ref):
    act = _silu(c_ref[...])
    o_ref[...] = jnp.dot(act, w_ref[...], preferred_element_type=F32) + b_ref[...]


def _ada(c_pad, w, b):
    rows, d = c_pad.shape
    n = w.shape[1]
    return pl.pallas_call(
        _ada_kernel,
        grid=(n // ADA_TN,),
        in_specs=[
            pl.BlockSpec((rows, d), lambda j: (0, 0)),
            pl.BlockSpec((d, ADA_TN), lambda j: (0, j)),
            pl.BlockSpec((1, ADA_TN), lambda j: (0, j)),
        ],
        out_specs=pl.BlockSpec((rows, ADA_TN), lambda j: (0, j)),
        out_shape=jax.ShapeDtypeStruct((rows, n), F32),
        compiler_params=_cparams(("arbitrary",)),
        name="ada",
    )(c_pad, w, b)


def _inv_freq_lanes(shape):
    lane = lax.broadcasted_iota(jnp.int32, shape, len(shape) - 1)
    f = (lane & (HEAD_DIM // 2 - 1)).astype(F32)
    return jnp.exp(f * (-math.log(ROPE_THETA) / (HEAD_DIM // 2)))


def _rope_tab_kernel(cos_ref, sin_ref):
    r = lax.broadcasted_iota(jnp.int32, (TM, LANES), 0).astype(F32)
    ang = r * _inv_freq_lanes((TM, LANES))
    cos_ref[...] = jnp.cos(ang)
    sin_ref[...] = jnp.sin(ang)


def _rope_tab():
    return pl.pallas_call(
        _rope_tab_kernel,
        out_shape=(jax.ShapeDtypeStruct((TM, LANES), F32),) * 2,
        name="rope_tab",
    )()


def _rpb_tab_kernel(rpb_ref, o_ref):
    p = pl.program_id(0)
    w = lax.broadcasted_iota(jnp.int32, (GRID_W, LANES), 0)
    lane = lax.broadcasted_iota(jnp.int32, (GRID_W, LANES), 1)
    c = lane & (GRID_W - 1)
    t = c - w + (NA_KC - 1)
    off = c - jnp.clip(w - NA_KC // 2, 0, GRID_W - NA_KC)
    n_rel = 2 * NA_KC - 1
    n_row = 2 * NA_KR - 1
    even_key_row = lane < GRID_W
    tiles = [[jnp.zeros((GRID_W, LANES), F32) for _ in range(n_row)] for _ in range(2)]
    for tt in range(n_rel):
        hit = t == tt
        for hh in range(2):
            for rr in range(n_row):
                val = rpb_ref[((2 * p + hh) * n_row + rr) * n_rel + tt] * LOG2E
                tiles[hh][rr] = jnp.where(hit, val, tiles[hh][rr])
    for hh in range(2):
        base = [jnp.where(off < NA_KC, jnp.where(off >= 0, x, NEG_BIG), NEG_BIG) for x in tiles[hh]]
        for d in range(NA_KR):
            for a in range(NA_KR // 2):
                lo = 2 * a - d + NA_KR - 1
                o_ref[d, hh * GRID_W:(hh + 1) * GRID_W, a * LANES:(a + 1) * LANES] = (
                    jnp.where(even_key_row, base[lo], base[lo + 1]))


def _rpb_tab(rpb_flat):
    pairs = NA_HEADS // 2
    shape = (NA_KR, pairs, 2 * GRID_W, NA_KR * GRID_W)
    return pl.pallas_call(
        _rpb_tab_kernel,
        grid=(pairs,),
        in_specs=[pl.BlockSpec(memory_space=pltpu.SMEM)],
        out_specs=pl.BlockSpec((NA_KR, None) + shape[2:], lambda p: (0, p, 0, 0)),
        out_shape=jax.ShapeDtypeStruct(shape, F32),
        compiler_params=_cparams(("arbitrary",)),
        name="rpb_tab",
    )(rpb_flat)


def _rope(x, cosv, sin_signed, first_half):
    swapped = jnp.where(first_half, pltpu.roll(x, LANES - HEAD_DIM // 2, 1),
                        pltpu.roll(x, HEAD_DIM // 2, 1))
    return x * cosv + swapped * sin_signed


def _inproj_kernel(x_ref, mod_ref, w_ref, cosb_ref, sinb_ref,
                   naq_ref, nak_ref, nav_ref, daq_ref, dak_ref, davt_ref):
    i = pl.program_id(1)
    shift1 = mod_ref[:, 0:D_MODEL]
    scale1 = mod_ref[:, D_MODEL:2 * D_MODEL]
    o1, o2, o3 = NA_WIDTH, 2 * NA_WIDTH, 3 * NA_WIDTH
    o4, o5 = o3 + DA_QK, o3 + 2 * DA_QK
    base = (i * TM).astype(F32) * _inv_freq_lanes((1, LANES))
    cosa, sina = jnp.cos(base), jnp.sin(base)
    lane = lax.broadcasted_iota(jnp.int32, (TM_SUB, LANES), 1)
    first_half = (lane & (HEAD_DIM - 1)) < HEAD_DIM // 2
    for t in range(TM // TM_SUB):
        rows = slice(t * TM_SUB, (t + 1) * TM_SUB)
        h = _ln(x_ref[rows, :]) * (1.0 + scale1) + shift1
        proj = jnp.dot(h.astype(BF16), w_ref[...].astype(BF16),
                       preferred_element_type=F32)
        naq_ref[rows, :] = (proj[:, :o1] * Q_SCALE).astype(BF16)
        nak_ref[rows, :] = proj[:, o1:o2].astype(BF16)
        nav_ref[rows, :] = proj[:, o2:o3].astype(BF16)
        cosb, sinb = cosb_ref[rows, :], sinb_ref[rows, :]
        cosv = cosa * cosb - sina * sinb
        sinv = sina * cosb + cosa * sinb
        sin_signed = jnp.where(first_half, -sinv, sinv)
        for g in range(DA_QK // LANES):
            sl = slice(g * LANES, (g + 1) * LANES)
            q = _rope(proj[:, o3 + g * LANES:o3 + (g + 1) * LANES], cosv, sin_signed, first_half)
            daq_ref[rows, sl] = (q * Q_SCALE).astype(BF16)
            k = _rope(proj[:, o4 + g * LANES:o4 + (g + 1) * LANES], cosv, sin_signed, first_half)
            dak_ref[rows, sl] = k.astype(BF16)
        davt_ref[:, rows] = proj[:, o5:].T.astype(BF16)


def _inproj(x, mod3, w_in, cosb, sinb):
    b, s, d = x.shape
    tok = lambda width: pl.BlockSpec((None, TM, width), lambda bi, i: (bi, i, 0))
    sds = lambda width: jax.ShapeDtypeStruct((b, s, width), BF16)
    return pl.pallas_call(
        _inproj_kernel,
        grid=(b, s // TM),
        in_specs=[
            tok(d),
            pl.BlockSpec((None, 1, mod3.shape[-1]), lambda bi, i: (bi, 0, 0)),
            pl.BlockSpec((d, IN_COLS), lambda bi, i: (0, 0), pipeline_mode=pl.Buffered(1)),
            pl.BlockSpec((TM, LANES), lambda bi, i: (0, 0)),
            pl.BlockSpec((TM, LANES), lambda bi, i: (0, 0)),
        ],
        out_specs=[tok(NA_WIDTH), tok(NA_WIDTH), tok(NA_WIDTH), tok(DA_QK), tok(DA_QK),
                   pl.BlockSpec((None, DA_WIDTH, TM), lambda bi, i: (bi, 0, i))],
        out_shape=[sds(NA_WIDTH), sds(NA_WIDTH), sds(NA_WIDTH), sds(DA_QK), sds(DA_QK),
                   jax.ShapeDtypeStruct((b, DA_WIDTH, s), BF16)],
        compiler_params=_cparams(("arbitrary", "arbitrary")),
        name="inproj",
    )(x, mod3, w_in, cosb, sinb)


def _nattn_kernel(q_ref, k_ref, v_ref, t_ref, o_ref, *, rows):
    n_keys = NA_KR * GRID_W
    lane = lax.broadcasted_iota(jnp.int32, (GRID_W, LANES), 1)
    first_head = lane < GRID_W
    ones = jnp.ones((n_keys, LANES), BF16)
    for i in range(NA_RB):
        r = pl.program_id(1) * NA_RB + i
        rs = jnp.clip(r - NA_KR // 2, 0, rows - NA_KR)
        d = r - rs
        start = pl.multiple_of(rs * GRID_W, GRID_W)
        tok = slice(i * GRID_W, (i + 1) * GRID_W)
        for p in range(NA_HEADS // 2):
            sl = slice(p * LANES, (p + 1) * LANES)
            q2 = q_ref[tok, sl]
            zero = jnp.zeros_like(q2)
            lhs = jnp.concatenate([jnp.where(first_head, q2, zero),
                                   jnp.where(first_head, zero, q2)], axis=0)
            k2 = k_ref[pl.ds(start, n_keys), sl]
            v2 = v_ref[pl.ds(start, n_keys), sl]
            s = lax.dot_general(lhs, k2, (((1,), (1,)), ((), ())),
                                preferred_element_type=F32)
            s = s + t_ref[d, p]
            e = jnp.exp2(s - jnp.max(s, axis=-1, keepdims=True)).astype(BF16)
            o2 = jnp.dot(e, jnp.concatenate([v2, ones], axis=1),
                         preferred_element_type=F32)
            o2 = o2[:, :LANES] * (1.0 / o2[:, LANES:])
            o_ref[tok, sl] = jnp.where(first_head, o2[:GRID_W], o2[GRID_W:]).astype(o_ref.dtype)


def _nattn(q, k, v, tab):
    b, s, width = q.shape
    rows = s // GRID_W
    once = pl.Buffered(1)
    full = pl.BlockSpec((None, s, width), lambda bi, r: (bi, 0, 0))
    row = pl.BlockSpec((None, NA_RB * GRID_W, width), lambda bi, r: (bi, r, 0))
    return pl.pallas_call(
        functools.partial(_nattn_kernel, rows=rows),
        grid=(b, rows // NA_RB),
        in_specs=[row, full, full,
                  pl.BlockSpec(tab.shape, lambda bi, r: (0, 0, 0, 0), pipeline_mode=once)],
        out_specs=row,
        out_shape=jax.ShapeDtypeStruct((b, s, width), BF16),
        compiler_params=_cparams(("arbitrary", "arbitrary")),
        name="nattn",
    )(q, k, v, tab)


def _da_scores(k_blk, rhs):
    return lax.dot_general(k_blk, rhs, (((1,), (1,)), ((), ())), preferred_element_type=F32)


def _da_output(acc1, acc2, l1, l2, lam, g):
    o = acc1 * (1.0 / l1) - lam * (acc2 * (1.0 / l2))
    ms = jnp.mean(o * o, axis=0, keepdims=True)
    o = o * lax.rsqrt(ms + LN_EPS)
    return o.T * g * (1.0 - LAMBDA_INIT)


def _da_stream(rhs, k_ref, vt_ref, acc_ref, n_keys):
    k0 = k_ref[0:DA_EST, :]
    ref8 = [jnp.broadcast_to(jnp.max(_da_scores(k0, r), axis=0, keepdims=True), (SUBLANES, TQ))
            for r in rhs]
    acc_ref[...] = jnp.zeros_like(acc_ref)

    def body(j, carry):
        out = []
        for mp in range(2):
            l8 = carry[mp]
            pv = None
            for c in range(TK // DA_SUB):
                start = pl.multiple_of(j * TK + c * DA_SUB, DA_SUB)
                s = _da_scores(k_ref[pl.ds(start, DA_SUB), :], rhs[mp])
                p = jnp.exp2(s.reshape(DA_SUB // SUBLANES, SUBLANES, TQ) - ref8[mp])
                l8 = l8 + jnp.sum(p, axis=0)
                d = jnp.dot(vt_ref[:, pl.ds(start, DA_SUB)],
                            p.reshape(DA_SUB, TQ).astype(BF16), preferred_element_type=F32)
                pv = d if pv is None else pv + d
            acc_ref[mp] += pv
            out.append(l8)
        return tuple(out)

    z8 = jnp.zeros((SUBLANES, TQ), F32)
    l8 = lax.fori_loop(0, n_keys // TK, body, (z8, z8))
    return [jnp.sum(l, axis=0, keepdims=True) for l in l8]


def _da_online(rhs, k_ref, vt_ref, acc_ref, n_keys):
    acc_ref[...] = jnp.zeros_like(acc_ref)

    def body(j, carry):
        start = pl.multiple_of(j * TK_EXACT, TK_EXACT)
        kb = k_ref[pl.ds(start, TK_EXACT), :]
        vb = vt_ref[:, pl.ds(start, TK_EXACT)]
        new = []
        for mp in range(2):
            m_old, l_old = carry[2 * mp], carry[2 * mp + 1]
            s = _da_scores(kb, rhs[mp])
            m_new = jnp.maximum(m_old, jnp.max(s, axis=0, keepdims=True))
            a = jnp.exp2(m_old - m_new)
            p = jnp.exp2(s - m_new)
            l_new = a * l_old + jnp.sum(p, axis=0, keepdims=True)
            acc_ref[mp] = acc_ref[mp] * a + jnp.dot(vb, p.astype(BF16),
                                                    preferred_element_type=F32)
            new += [m_new, l_new]
        return tuple(new)

    m0 = jnp.full((1, TQ), -jnp.inf, F32)
    l0 = jnp.zeros((1, TQ), F32)
    _, l1, _, l2 = lax.fori_loop(0, n_keys // TK_EXACT, body, (m0, l0, m0, l0))
    return [l1, l2]


def _dattn_kernel(q_ref, k_ref, vt_ref, lam_ref, g_ref, *rest, n_keys, n_cast):
    w_refs, (o_ref, *wb_refs, acc_ref) = rest[:n_cast], rest[n_cast:]
    for w_ref, wb_ref in zip(w_refs, wb_refs):
        wb_ref[...] = w_ref[...].astype(wb_ref.dtype)
    q = q_ref[...]
    lane = lax.broadcasted_iota(jnp.int32, q.shape, 1)
    zero = jnp.zeros_like(q)
    rhs = (jnp.where(lane < HEAD_DIM, q, zero), jnp.where(lane < HEAD_DIM, zero, q))
    lam = (jnp.exp(jnp.sum(lam_ref[0:1, :] * lam_ref[1:2, :], axis=-1, keepdims=True))
           - jnp.exp(jnp.sum(lam_ref[2:3, :] * lam_ref[3:4, :], axis=-1, keepdims=True))
           + LAMBDA_INIT)

    l1, l2 = _da_stream(rhs, k_ref, vt_ref, acc_ref, n_keys)
    out = _da_output(acc_ref[0], acc_ref[1], l1, l2, lam, g_ref[...])
    o_ref[...] = out.astype(o_ref.dtype)
    overflowed = (jnp.sum(jnp.where(jnp.isfinite(out), 0.0, 1.0))
                  + jnp.sum(jnp.where(jnp.isfinite(l1 + l2), 0.0, 1.0)))

    @pl.when(overflowed > 0.0)
    def _():
        e1, e2 = _da_online(rhs, k_ref, vt_ref, acc_ref, n_keys)
        o_ref[...] = _da_output(acc_ref[0], acc_ref[1], e1, e2, lam,
                                g_ref[...]).astype(o_ref.dtype)


def _cast_spec(w, n_q, n_steps):
    rows = w.shape[0]
    n_blocks = max(n for n in range(1, n_steps + 1)
                   if n_steps % n == 0 and rows % (n * BF16_SUBLANES) == 0)
    rep = n_steps // n_blocks
    return pl.BlockSpec((rows // n_blocks, w.shape[1]),
                        lambda bi, h, i: (((bi * DA_HEADS + h) * n_q + i) // rep, 0))


def _dattn(q, k, vt, lam4, g, weights):
    b, s, _ = q.shape
    n_q = s // TQ
    w_specs = [_cast_spec(w, n_q, b * DA_HEADS * n_q) for w in weights]
    return pl.pallas_call(
        functools.partial(_dattn_kernel, n_keys=s, n_cast=len(weights)),
        grid=(b, DA_HEADS, n_q),
        in_specs=[
            pl.BlockSpec((None, TQ, LANES), lambda bi, h, i: (bi, i, h)),
            pl.BlockSpec((None, s, LANES), lambda bi, h, i: (bi, 0, h)),
            pl.BlockSpec((None, DA_VDIM, s), lambda bi, h, i: (bi, h, 0)),
            pl.BlockSpec(lam4.shape, lambda bi, h, i: (0, 0)),
            pl.BlockSpec(g.shape, lambda bi, h, i: (0, 0)),
        ] + w_specs,
        out_specs=[pl.BlockSpec((None, TQ, DA_VDIM), lambda bi, h, i: (bi, i, h))] + w_specs,
        out_shape=[jax.ShapeDtypeStruct((b, s, DA_WIDTH), BF16)]
        + [jax.ShapeDtypeStruct(w.shape, BF16) for w in weights],
        scratch_shapes=[pltpu.VMEM((2, DA_VDIM, TQ), F32)],
        compiler_params=_cparams(("arbitrary", "arbitrary", "arbitrary")),
        name="dattn",
    )(q, k, vt, lam4, g, *weights)


def _mlp_kernel(oa_ref, ob_ref, x_ref, mod_ref, wo_ref, wg_ref, wu_ref, wd_ref,
                g1_ref, b1_ref, g2_ref, b2_ref, o_ref):
    gate1 = mod_ref[:, 2 * D_MODEL:3 * D_MODEL]
    shift2 = mod_ref[:, 3 * D_MODEL:4 * D_MODEL]
    scale2 = mod_ref[:, 4 * D_MODEL:5 * D_MODEL]
    gate2 = mod_ref[:, 5 * D_MODEL:6 * D_MODEL]
    n_sub = TM_MLP // MLP_SUB
    rows = [slice(t * MLP_SUB, (t + 1) * MLP_SUB) for t in range(n_sub)]

    def attn_out(r):
        return (jnp.dot(oa_ref[r, :], wo_ref[0:NA_WIDTH, :], preferred_element_type=F32)
                + jnp.dot(ob_ref[r, :], wo_ref[NA_WIDTH:, :], preferred_element_type=F32))

    def norm1(r, mix):
        x1 = _ln(ALPHA * x_ref[r, :] + (1.0 + gate1) * mix) * g1_ref[...] + b1_ref[...]
        return x1, (_ln(x1) * (1.0 + scale2) + shift2).astype(BF16)

    def swiglu(h2):
        ffn = None
        for c0, c1 in zip(FF_SPLITS[:-1], FF_SPLITS[1:]):
            gate = jnp.dot(h2, wg_ref[:, c0:c1], preferred_element_type=F32)
            up = jnp.dot(h2, wu_ref[:, c0:c1], preferred_element_type=F32)
            act = (_silu(gate) * up).astype(BF16)
            part = jnp.dot(act, wd_ref[c0:c1, :], preferred_element_type=F32)
            ffn = part if ffn is None else ffn + part
        return ffn

    def norm2(r, x1, ffn):
        o_ref[r, :] = _ln(ALPHA * x1 + (1.0 + gate2) * ffn) * g2_ref[...] + b2_ref[...]

    mix = [attn_out(r) for r in rows]
    x1h2 = [None] * n_sub
    ffn = [None] * n_sub
    x1h2[0] = norm1(rows[0], mix[0])
    for t in range(n_sub):
        ffn[t] = swiglu(x1h2[t][1])
        if t + 1 < n_sub:
            x1h2[t + 1] = norm1(rows[t + 1], mix[t + 1])
        if t > 0:
            norm2(rows[t - 1], x1h2[t - 1][0], ffn[t - 1])
    norm2(rows[-1], x1h2[-1][0], ffn[-1])


def _mlp(oa, ob, x, mod3, w_out, wg, wu, wd, g1, b1, g2, b2):
    b, s, d = x.shape
    assert FF_SPLITS[0] == 0 and FF_SPLITS[-1] == wg.shape[1]
    tok = lambda width: pl.BlockSpec((None, TM_MLP, width), lambda bi, i: (bi, i, 0))
    vec = pl.BlockSpec((1, d), lambda bi, i: (0, 0))
    once = lambda shape: pl.BlockSpec(shape, lambda bi, i: (0, 0), pipeline_mode=pl.Buffered(1))
    return pl.pallas_call(
        _mlp_kernel,
        grid=(b, s // TM_MLP),
        in_specs=[tok(NA_WIDTH), tok(DA_WIDTH), tok(d),
                  pl.BlockSpec((None, 1, mod3.shape[-1]), lambda bi, i: (bi, 0, 0)),
                  once(w_out.shape), once(wg.shape), once(wu.shape), once(wd.shape),
                  vec, vec, vec, vec],
        out_specs=tok(d),
        out_shape=jax.ShapeDtypeStruct((b, s, d), F32),
        compiler_params=_cparams(("arbitrary", "arbitrary")),
        name="mlp",
    )(oa, ob, x, mod3, w_out, wg, wu, wd, g1, b1, g2, b2)


def kernel(x, c, w_ada, b_ada, w_in, rpb, lambda_q1, lambda_k1, lambda_q2, lambda_k2,
           subln_g, w_out, ln1_g, ln1_b, w_gate, w_up, w_down, ln2_g, ln2_b):
    b, s, d = x.shape
    assert d == D_MODEL and s % TM == 0 and s % TM_MLP == 0 and s % GRID_W == 0
    assert s % TQ == 0 and s % TK == 0 and TM % TM_SUB == 0 and TM_MLP % MLP_SUB == 0
    assert s % TK_EXACT == 0 and TK % DA_SUB == 0
    assert w_ada.shape[0] == DEPTH == 1
    l = 0
    sub = 8
    c_pad = jnp.zeros((sub, d), F32).at[:b].set(c)
    mod = _ada(c_pad, w_ada[l], b_ada[l].reshape(1, -1))[:b]
    mod3 = mod.reshape(b, 1, -1)

    cosb, sinb = _rope_tab()
    tab = _rpb_tab(rpb[l].reshape(-1))

    naq, nak, nav, daq, dak, davt = _inproj(x, mod3, w_in[l], cosb, sinb)
    out_a = _nattn(naq, nak, nav, tab)
    lam4 = jnp.stack([lambda_q1[l], lambda_k1[l], lambda_q2[l], lambda_k2[l]])
    out_b, wo, wg, wu, wd = _dattn(daq, dak, davt, lam4, subln_g[l].reshape(1, -1),
                                   (w_out[l], w_gate[l], w_up[l], w_down[l]))
    return _mlp(out_a, out_b, x, mod3, wo, wg, wu, wd,
                ln1_g[l].reshape(1, -1), ln1_b[l].reshape(1, -1),
                ln2_g[l].reshape(1, -1), ln2_b[l].reshape(1, -1))
```

```python
import functools
import math

import jax
import jax.numpy as jnp
from jax import lax
from jax.experimental import pallas as pl
from jax.experimental.pallas import tpu as pltpu

D_MODEL = 1024
GRID_W = 64
HEAD_DIM = 64
NA_HEADS = 8
NA_WIDTH = NA_HEADS * HEAD_DIM
NA_KR = 8
NA_KC = 16
DA_HEADS = 4
DA_VDIM = 2 * HEAD_DIM
DA_QK = DA_HEADS * 2 * HEAD_DIM
DA_WIDTH = DA_HEADS * DA_VDIM
IN_COLS = 3 * NA_WIDTH + 2 * DA_QK + DA_WIDTH
ROPE_THETA = 10000.0
LN_EPS = 1e-5
DEPTH = 1
ALPHA = (2.0 * DEPTH) ** 0.25
LAMBDA_INIT = 0.8 - 0.6 * math.exp(-0.3 * 0)
LOG2E = math.log2(math.e)
Q_SCALE = HEAD_DIM ** -0.5 * LOG2E

LANES = 128
SUBLANES = 8
BF16_SUBLANES = 16
V7X_VMEM_BYTES = 64 * 1024 * 1024
NEG_BIG = -1e30

BF16 = jnp.bfloat16
F32 = jnp.float32

TM = 1024
TM_SUB = 512
TM_MLP = 1024
MLP_SUB = 256
FF_SPLITS = (0, 1536, 2816)
NA_RB = 16
TQ = 2048
TK = 8192
DA_SUB = 2048
TK_EXACT = 512
DA_EST = 128
ADA_TN = 1024
VMEM_LIMIT = 56 * 1024 * 1024


def _cparams(sem):
    return pltpu.CompilerParams(dimension_semantics=sem, vmem_limit_bytes=VMEM_LIMIT)


def _ln(x):
    mu = jnp.mean(x, axis=-1, keepdims=True)
    xc = x - mu
    var = jnp.mean(xc * xc, axis=-1, keepdims=True)
    return xc * lax.rsqrt(var + LN_EPS)


def _silu(x):
    return x / (1.0 + jnp.exp(-x))


def _ada_kernel(c_ref, w_ref, b_ref, o_ref):
    act = _silu(c_ref[...])
    o_ref[...] = jnp.dot(act, w_ref[...], preferred_element_type=F32) + b_ref[...]


def _ada(c_pad, w, b, n):
    rows, d = c_pad.shape
    return pl.pallas_call(
        _ada_kernel,
        grid=(n // ADA_TN,),
        in_specs=[
            pl.BlockSpec((rows, d), lambda j: (0, 0)),
            pl.BlockSpec((d, ADA_TN), lambda j: (0, j)),
            pl.BlockSpec((1, ADA_TN), lambda j: (0, j)),
        ],
        out_specs=pl.BlockSpec((rows, ADA_TN), lambda j: (0, j)),
        out_shape=jax.ShapeDtypeStruct((rows, n), F32),
        compiler_params=_cparams(("arbitrary",)),
        name="ada",
    )(c_pad, w, b)


def _inv_freq_lanes(shape):
    lane = lax.broadcasted_iota(jnp.int32, shape, len(shape) - 1)
    f = (lane & (HEAD_DIM // 2 - 1)).astype(F32)
    return jnp.exp(f * (-math.log(ROPE_THETA) / (HEAD_DIM // 2)))


def _rope_tab_kernel(cos_ref, sin_ref):
    r = lax.broadcasted_iota(jnp.int32, (TM, LANES), 0).astype(F32)
    ang = r * _inv_freq_lanes((TM, LANES))
    cos_ref[...] = jnp.cos(ang)
    sin_ref[...] = jnp.sin(ang)


def _rope_tab():
    return pl.pallas_call(
        _rope_tab_kernel,
        out_shape=(jax.ShapeDtypeStruct((TM, LANES), F32),) * 2,
        name="rope_tab",
    )()


def _rpb_tab_kernel(rpb_ref, o_ref):
    p = pl.program_id(0)
    w = lax.broadcasted_iota(jnp.int32, (GRID_W, LANES), 0)
    lane = lax.broadcasted_iota(jnp.int32, (GRID_W, LANES), 1)
    c = lane & (GRID_W - 1)
    t = c - w + (NA_KC - 1)
    off = c - jnp.clip(w - NA_KC // 2, 0, GRID_W - NA_KC)
    n_rel = 2 * NA_KC - 1
    n_row = 2 * NA_KR - 1
    even_key_row = lane < GRID_W
    tiles = [[jnp.zeros((GRID_W, LANES), F32) for _ in range(n_row)] for _ in range(2)]
    for tt in range(n_rel):
        hit = t == tt
        for hh in range(2):
            for rr in range(n_row):
                val = rpb_ref[((2 * p + hh) * n_row + rr) * n_rel + tt] * LOG2E
                tiles[hh][rr] = jnp.where(hit, val, tiles[hh][rr])
    for hh in range(2):
        base = [jnp.where(off < NA_KC, jnp.where(off >= 0, x, NEG_BIG), NEG_BIG) for x in tiles[hh]]
        for d in range(NA_KR):
            for a in range(NA_KR // 2):
                lo = 2 * a - d + NA_KR - 1
                o_ref[d, hh * GRID_W:(hh + 1) * GRID_W, a * LANES:(a + 1) * LANES] = (
                    jnp.where(even_key_row, base[lo], base[lo + 1]))


def _rpb_tab(rpb_flat):
    pairs = NA_HEADS // 2
    shape = (NA_KR, pairs, 2 * GRID_W, NA_KR * GRID_W)
    return pl.pallas_call(
        _rpb_tab_kernel,
        grid=(pairs,),
        in_specs=[pl.BlockSpec(memory_space=pltpu.SMEM)],
        out_specs=pl.BlockSpec((NA_KR, None) + shape[2:], lambda p: (0, p, 0, 0)),
        out_shape=jax.ShapeDtypeStruct(shape, F32),
        compiler_params=_cparams(("arbitrary",)),
        name="rpb_tab",
    )(rpb_flat)


def _rope(x, cosv, sin_signed, first_half):
    swapped = jnp.where(first_half, pltpu.roll(x, LANES - HEAD_DIM // 2, 1),
                        pltpu.roll(x, HEAD_DIM // 2, 1))
    return x * cosv + swapped * sin_signed


def _inproj_kernel(x_ref, mod_ref, w_ref, cosb_ref, sinb_ref,
                   naq_ref, nak_ref, nav_ref, daq_ref, dak_ref, davt_ref):
    i = pl.program_id(1)
    shift1 = mod_ref[:, 0:D_MODEL]
    scale1 = mod_ref[:, D_MODEL:2 * D_MODEL]
    o1, o2, o3 = NA_WIDTH, 2 * NA_WIDTH, 3 * NA_WIDTH
    o4, o5 = o3 + DA_QK, o3 + 2 * DA_QK
    base = (i * TM).astype(F32) * _inv_freq_lanes((1, LANES))
    cosa, sina = jnp.cos(base), jnp.sin(base)
    lane = lax.broadcasted_iota(jnp.int32, (TM_SUB, LANES), 1)
    first_half = (lane & (HEAD_DIM - 1)) < HEAD_DIM // 2
    for t in range(TM // TM_SUB):
        rows = slice(t * TM_SUB, (t + 1) * TM_SUB)
        h = _ln(x_ref[rows, :]) * (1.0 + scale1) + shift1
        proj = jnp.dot(h.astype(BF16), w_ref[...].astype(BF16),
                       preferred_element_type=F32)
        naq_ref[rows, :] = (proj[:, :o1] * Q_SCALE).astype(BF16)
        nak_ref[rows, :] = proj[:, o1:o2].astype(BF16)
        nav_ref[rows, :] = proj[:, o2:o3].astype(BF16)
        cosb, sinb = cosb_ref[rows, :], sinb_ref[rows, :]
        cosv = cosa * cosb - sina * sinb
        sinv = sina * cosb + cosa * sinb
        sin_signed = jnp.where(first_half, -sinv, sinv)
        for g in range(DA_QK // LANES):
            sl = slice(g * LANES, (g + 1) * LANES)
            q = _rope(proj[:, o3 + g * LANES:o3 + (g + 1) * LANES], cosv, sin_signed, first_half)
            daq_ref[rows, sl] = (q * Q_SCALE).astype(BF16)
            k = _rope(proj[:, o4 + g * LANES:o4 + (g + 1) * LANES], cosv, sin_signed, first_half)
            dak_ref[rows, sl] = k.astype(BF16)
        davt_ref[:, rows] = proj[:, o5:].T.astype(BF16)


def _inproj(x, mod3, w_in, cosb, sinb):
    b, s, d = x.shape
    tok = lambda width: pl.BlockSpec((None, TM, width), lambda bi, i: (bi, i, 0))
    sds = lambda width: jax.ShapeDtypeStruct((b, s, width), BF16)
    return pl.pallas_call(
        _inproj_kernel,
        grid=(b, s // TM),
        in_specs=[
            tok(d),
            pl.BlockSpec((None, 1, mod3.shape[-1]), lambda bi, i: (bi, 0, 0)),
            pl.BlockSpec((d, IN_COLS), lambda bi, i: (0, 0), pipeline_mode=pl.Buffered(1)),
            pl.BlockSpec((TM, LANES), lambda bi, i: (0, 0)),
            pl.BlockSpec((TM, LANES), lambda bi, i: (0, 0)),
        ],
        out_specs=[tok(NA_WIDTH), tok(NA_WIDTH), tok(NA_WIDTH), tok(DA_QK), tok(DA_QK),
                   pl.BlockSpec((None, DA_WIDTH, TM), lambda bi, i: (bi, 0, i))],
        out_shape=[sds(NA_WIDTH), sds(NA_WIDTH), sds(NA_WIDTH), sds(DA_QK), sds(DA_QK),
                   jax.ShapeDtypeStruct((b, DA_WIDTH, s), BF16)],
        compiler_params=_cparams(("arbitrary", "arbitrary")),
        name="inproj",
    )(x, mod3, w_in, cosb, sinb)


def _nattn_kernel(q_ref, k_ref, v_ref, t_ref, o_ref, *, rows):
    n_keys = NA_KR * GRID_W
    lane = lax.broadcasted_iota(jnp.int32, (GRID_W, LANES), 1)
    first_head = lane < GRID_W
    ones = jnp.ones((n_keys, LANES), BF16)
    for i in range(NA_RB):
        r = pl.program_id(1) * NA_RB + i
        rs = jnp.clip(r - NA_KR // 2, 0, rows - NA_KR)
        d = r - rs
        start = pl.multiple_of(rs * GRID_W, GRID_W)
        tok = slice(i * GRID_W, (i + 1) * GRID_W)
        for p in range(NA_HEADS // 2):
            sl = slice(p * LANES, (p + 1) * LANES)
            q2 = q_ref[tok, sl]
            zero = jnp.zeros_like(q2)
            lhs = jnp.concatenate([jnp.where(first_head, q2, zero),
                                   jnp.where(first_head, zero, q2)], axis=0)
            k2 = k_ref[pl.ds(start, n_keys), sl]
            v2 = v_ref[pl.ds(start, n_keys), sl]
            s = lax.dot_general(lhs, k2, (((1,), (1,)), ((), ())),
                                preferred_element_type=F32)
            s = s + t_ref[d, p]
            e = jnp.exp2(s - jnp.max(s, axis=-1, keepdims=True)).astype(BF16)
            o2 = jnp.dot(e, jnp.concatenate([v2, ones], axis=1),
                         preferred_element_type=F32)
            o2 = o2[:, :LANES] * (1.0 / o2[:, LANES:])
            o_ref[tok, sl] = jnp.where(first_head, o2[:GRID_W], o2[GRID_W:]).astype(o_ref.dtype)


def _nattn(q, k, v, tab):
    b, s, width = q.shape
    rows = s // GRID_W
    once = pl.Buffered(1)
    full = pl.BlockSpec((None, s, width), lambda bi, r: (bi, 0, 0))
    row = pl.BlockSpec((None, NA_RB * GRID_W, width), lambda bi, r: (bi, r, 0))
    return pl.pallas_call(
        functools.partial(_nattn_kernel, rows=rows),
        grid=(b, rows // NA_RB),
        in_specs=[row, full, full,
                  pl.BlockSpec(tab.shape, lambda bi, r: (0, 0, 0, 0), pipeline_mode=once)],
        out_specs=row,
        out_shape=jax.ShapeDtypeStruct((b, s, width), BF16),
        compiler_params=_cparams(("arbitrary", "arbitrary")),
        name="nattn",
    )(q, k, v, tab)


def _da_scores(k_blk, rhs):
    return lax.dot_general(k_blk, rhs, (((1,), (1,)), ((), ())), preferred_element_type=F32)


def _da_output(acc1, acc2, l1, l2, lam, g):
    o = acc1 * (1.0 / l1) - lam * (acc2 * (1.0 / l2))
    ms = jnp.mean(o * o, axis=0, keepdims=True)
    o = o * lax.rsqrt(ms + LN_EPS)
    return o.T * g * (1.0 - LAMBDA_INIT)


def _da_stream(rhs, k_ref, vt_ref, acc_ref, n_keys):
    k0 = k_ref[0:DA_EST, :]
    ref8 = [jnp.broadcast_to(jnp.max(_da_scores(k0, r), axis=0, keepdims=True), (SUBLANES, TQ))
            for r in rhs]
    acc_ref[...] = jnp.zeros_like(acc_ref)

    def body(j, carry):
        out = []
        for mp in range(2):
            l8 = carry[mp]
            pv = None
            for c in range(TK // DA_SUB):
                start = pl.multiple_of(j * TK + c * DA_SUB, DA_SUB)
                s = _da_scores(k_ref[pl.ds(start, DA_SUB), :], rhs[mp])
                p = jnp.exp2(s.reshape(DA_SUB // SUBLANES, SUBLANES, TQ) - ref8[mp])
                l8 = l8 + jnp.sum(p, axis=0)
                d = jnp.dot(vt_ref[:, pl.ds(start, DA_SUB)],
                            p.reshape(DA_SUB, TQ).astype(BF16), preferred_element_type=F32)
                pv = d if pv is None else pv + d
            acc_ref[mp] += pv
            out.append(l8)
        return tuple(out)

    z8 = jnp.zeros((SUBLANES, TQ), F32)
    l8 = lax.fori_loop(0, n_keys // TK, body, (z8, z8))
    return [jnp.sum(l, axis=0, keepdims=True) for l in l8]


def _da_online(rhs, k_ref, vt_ref, acc_ref, n_keys):
    acc_ref[...] = jnp.zeros_like(acc_ref)

    def body(j, carry):
        start = pl.multiple_of(j * TK_EXACT, TK_EXACT)
        kb = k_ref[pl.ds(start, TK_EXACT), :]
        vb = vt_ref[:, pl.ds(start, TK_EXACT)]
        new = []
        for mp in range(2):
            m_old, l_old = carry[2 * mp], carry[2 * mp + 1]
            s = _da_scores(kb, rhs[mp])
            m_new = jnp.maximum(m_old, jnp.max(s, axis=0, keepdims=True))
            a = jnp.exp2(m_old - m_new)
            p = jnp.exp2(s - m_new)
            l_new = a * l_old + jnp.sum(p, axis=0, keepdims=True)
            acc_ref[mp] = acc_ref[mp] * a + jnp.dot(vb, p.astype(BF16),
                                                    preferred_element_type=F32)
            new += [m_new, l_new]
        return tuple(new)

    m0 = jnp.full((1, TQ), -jnp.inf, F32)
    l0 = jnp.zeros((1, TQ), F32)
    _, l1, _, l2 = lax.fori_loop(0, n_keys // TK_EXACT, body, (m0, l0, m0, l0))
    return [l1, l2]


def _dattn_kernel(q_ref, k_ref, vt_ref, lam_ref, g_ref, c_ref, wa_ref, ba_ref, *rest, n_keys, n_cast):
    w_refs, (o_ref, mod_ref, *wb_refs, acc_ref) = rest[:n_cast], rest[n_cast:]
    mod_ref[...] = jnp.dot(_silu(c_ref[...]), wa_ref[...], preferred_element_type=F32) + ba_ref[...]
    for w_ref, wb_ref in zip(w_refs, wb_refs):
        wb_ref[...] = w_ref[...].astype(wb_ref.dtype)
    q = q_ref[...]
    lane = lax.broadcasted_iota(jnp.int32, q.shape, 1)
    zero = jnp.zeros_like(q)
    rhs = (jnp.where(lane < HEAD_DIM, q, zero), jnp.where(lane < HEAD_DIM, zero, q))
    lam = (jnp.exp(jnp.sum(lam_ref[0:1, :] * lam_ref[1:2, :], axis=-1, keepdims=True))
           - jnp.exp(jnp.sum(lam_ref[2:3, :] * lam_ref[3:4, :], axis=-1, keepdims=True))
           + LAMBDA_INIT)

    l1, l2 = _da_stream(rhs, k_ref, vt_ref, acc_ref, n_keys)
    out = _da_output(acc_ref[0], acc_ref[1], l1, l2, lam, g_ref[...])
    o_ref[...] = out.astype(o_ref.dtype)
    overflowed = (jnp.sum(jnp.where(jnp.isfinite(out), 0.0, 1.0))
                  + jnp.sum(jnp.where(jnp.isfinite(l1 + l2), 0.0, 1.0)))

    @pl.when(overflowed > 0.0)
    def _():
        e1, e2 = _da_online(rhs, k_ref, vt_ref, acc_ref, n_keys)
        o_ref[...] = _da_output(acc_ref[0], acc_ref[1], e1, e2, lam,
                                g_ref[...]).astype(o_ref.dtype)


def _cast_spec(w, n_q, n_steps):
    rows = w.shape[0]
    n_blocks = max(n for n in range(1, n_steps + 1)
                   if n_steps % n == 0 and rows % (n * BF16_SUBLANES) == 0)
    rep = n_steps // n_blocks
    return pl.BlockSpec((rows // n_blocks, w.shape[1]),
                        lambda bi, h, i: (((bi * DA_HEADS + h) * n_q + i) // rep, 0))


def _dattn(q, k, vt, lam4, g, c_pad, w_ada, b_ada, mod_from, weights):
    b, s, _ = q.shape
    n_q = s // TQ
    n_steps = b * DA_HEADS * n_q
    w_specs = [_cast_spec(w, n_q, n_steps) for w in weights]
    n_mod = (w_ada.shape[1] - mod_from) // n_steps
    assert n_mod % LANES == 0 and mod_from % n_mod == 0
    step = lambda bi, h, i: (bi * DA_HEADS + h) * n_q + i
    ada_col = lambda bi, h, i: (0, mod_from // n_mod + step(bi, h, i))
    mod_spec = pl.BlockSpec((c_pad.shape[0], n_mod), lambda bi, h, i: (0, step(bi, h, i)))
    return pl.pallas_call(
        functools.partial(_dattn_kernel, n_keys=s, n_cast=len(weights)),
        grid=(b, DA_HEADS, n_q),
        in_specs=[
            pl.BlockSpec((None, TQ, LANES), lambda bi, h, i: (bi, i, h)),
            pl.BlockSpec((None, s, LANES), lambda bi, h, i: (bi, 0, h)),
            pl.BlockSpec((None, DA_VDIM, s), lambda bi, h, i: (bi, h, 0)),
            pl.BlockSpec(lam4.shape, lambda bi, h, i: (0, 0)),
            pl.BlockSpec(g.shape, lambda bi, h, i: (0, 0)),
            pl.BlockSpec(c_pad.shape, lambda bi, h, i: (0, 0)),
            pl.BlockSpec((w_ada.shape[0], n_mod), ada_col),
            pl.BlockSpec((1, n_mod), ada_col),
        ] + w_specs,
        out_specs=[pl.BlockSpec((None, TQ, DA_VDIM), lambda bi, h, i: (bi, i, h)), mod_spec] + w_specs,
        out_shape=[jax.ShapeDtypeStruct((b, s, DA_WIDTH), BF16),
                   jax.ShapeDtypeStruct((c_pad.shape[0], w_ada.shape[1] - mod_from), F32)]
        + [jax.ShapeDtypeStruct(w.shape, BF16) for w in weights],
        scratch_shapes=[pltpu.VMEM((2, DA_VDIM, TQ), F32)],
        compiler_params=_cparams(("arbitrary", "arbitrary", "arbitrary")),
        name="dattn",
    )(q, k, vt, lam4, g, c_pad, w_ada, b_ada, *weights)


def _mlp_kernel(oa_ref, ob_ref, x_ref, mod_ref, wo_ref, wg_ref, wu_ref, wd_ref,
                g1_ref, b1_ref, g2_ref, b2_ref, o_ref):
    gate1 = mod_ref[:, 0:D_MODEL]
    shift2 = mod_ref[:, D_MODEL:2 * D_MODEL]
    scale2 = mod_ref[:, 2 * D_MODEL:3 * D_MODEL]
    gate2 = mod_ref[:, 3 * D_MODEL:4 * D_MODEL]
    n_sub = TM_MLP // MLP_SUB
    rows = [slice(t * MLP_SUB, (t + 1) * MLP_SUB) for t in range(n_sub)]

    def attn_out(r):
        return (jnp.dot(oa_ref[r, :], wo_ref[0:NA_WIDTH, :], preferred_element_type=F32)
                + jnp.dot(ob_ref[r, :], wo_ref[NA_WIDTH:, :], preferred_element_type=F32))

    def norm1(r, mix):
        x1 = _ln(ALPHA * x_ref[r, :] + (1.0 + gate1) * mix) * g1_ref[...] + b1_ref[...]
        return x1, (_ln(x1) * (1.0 + scale2) + shift2).astype(BF16)

    def swiglu(h2):
        ffn = None
        for c0, c1 in zip(FF_SPLITS[:-1], FF_SPLITS[1:]):
            gate = jnp.dot(h2, wg_ref[:, c0:c1], preferred_element_type=F32)
            up = jnp.dot(h2, wu_ref[:, c0:c1], preferred_element_type=F32)
            act = (_silu(gate) * up).astype(BF16)
            part = jnp.dot(act, wd_ref[c0:c1, :], preferred_element_type=F32)
            ffn = part if ffn is None else ffn + part
        return ffn

    def norm2(r, x1, ffn):
        o_ref[r, :] = _ln(ALPHA * x1 + (1.0 + gate2) * ffn) * g2_ref[...] + b2_ref[...]

    mix = [attn_out(r) for r in rows]
    x1h2 = [None] * n_sub
    ffn = [None] * n_sub
    x1h2[0] = norm1(rows[0], mix[0])
    for t in range(n_sub):
        ffn[t] = swiglu(x1h2[t][1])
        if t + 1 < n_sub:
            x1h2[t + 1] = norm1(rows[t + 1], mix[t + 1])
        if t > 0:
            norm2(rows[t - 1], x1h2[t - 1][0], ffn[t - 1])
    norm2(rows[-1], x1h2[-1][0], ffn[-1])


def _mlp(oa, ob, x, mod3, w_out, wg, wu, wd, g1, b1, g2, b2):
    b, s, d = x.shape
    assert FF_SPLITS[0] == 0 and FF_SPLITS[-1] == wg.shape[1]
    tok = lambda width: pl.BlockSpec((None, TM_MLP, width), lambda bi, i: (bi, i, 0))
    vec = pl.BlockSpec((1, d), lambda bi, i: (0, 0))
    once = lambda shape: pl.BlockSpec(shape, lambda bi, i: (0, 0), pipeline_mode=pl.Buffered(1))
    return pl.pallas_call(
        _mlp_kernel,
        grid=(b, s // TM_MLP),
        in_specs=[tok(NA_WIDTH), tok(DA_WIDTH), tok(d),
                  pl.BlockSpec((None, 1, mod3.shape[-1]), lambda bi, i: (bi, 0, 0)),
                  once(w_out.shape), once(wg.shape), once(wu.shape), once(wd.shape),
                  vec, vec, vec, vec],
        out_specs=tok(d),
        out_shape=jax.ShapeDtypeStruct((b, s, d), F32),
        compiler_params=_cparams(("arbitrary", "arbitrary")),
        name="mlp",
    )(oa, ob, x, mod3, w_out, wg, wu, wd, g1, b1, g2, b2)


def kernel(x, c, w_ada, b_ada, w_in, rpb, lambda_q1, lambda_k1, lambda_q2, lambda_k2,
           subln_g, w_out, ln1_g, ln1_b, w_gate, w_up, w_down, ln2_g, ln2_b):
    b, s, d = x.shape
    assert d == D_MODEL and s % TM == 0 and s % TM_MLP == 0 and s % GRID_W == 0
    assert s % TQ == 0 and s % TK == 0 and TM % TM_SUB == 0 and TM_MLP % MLP_SUB == 0
    assert s % TK_EXACT == 0 and TK % DA_SUB == 0
    assert w_ada.shape[0] == DEPTH == 1
    l = 0
    sub = 8
    c_pad = jnp.zeros((sub, d), F32).at[:b].set(c)
    n_mod1 = 2 * d
    b_ada2 = b_ada[l].reshape(1, -1)
    mod1 = _ada(c_pad, w_ada[l], b_ada2, n_mod1)[:b].reshape(b, 1, -1)

    cosb, sinb = _rope_tab()
    tab = _rpb_tab(rpb[l].reshape(-1))

    naq, nak, nav, daq, dak, davt = _inproj(x, mod1, w_in[l], cosb, sinb)
    out_a = _nattn(naq, nak, nav, tab)
    lam4 = jnp.stack([lambda_q1[l], lambda_k1[l], lambda_q2[l], lambda_k2[l]])
    out_b, mod2, wo, wg, wu, wd = _dattn(daq, dak, davt, lam4, subln_g[l].reshape(1, -1),
                                         c_pad, w_ada[l], b_ada2, n_mod1,
                                         (w_out[l], w_gate[l], w_up[l], w_down[l]))
    return _mlp(out_a, out_b, x, mod2[:b].reshape(b, 1, -1), wo, wg, wu, wd,
                ln1_g[l].reshape(1, -1), ln1_b[l].reshape(1, -1),
                ln2_g[l].reshape(1, -1), ln2_b[l].reshape(1, -1))
```

```python
import functools
import math

import jax
import jax.numpy as jnp
from jax import lax
from jax.experimental import pallas as pl
from jax.experimental.pallas import tpu as pltpu

D_MODEL = 1024
GRID_W = 64
HEAD_DIM = 64
NA_HEADS = 8
NA_WIDTH = NA_HEADS * HEAD_DIM
NA_KR = 8
NA_KC = 16
DA_HEADS = 4
DA_VDIM = 2 * HEAD_DIM
DA_QK = DA_HEADS * 2 * HEAD_DIM
DA_WIDTH = DA_HEADS * DA_VDIM
IN_COLS = 3 * NA_WIDTH + 2 * DA_QK + DA_WIDTH
ROPE_THETA = 10000.0
LN_EPS = 1e-5
DEPTH = 1
ALPHA = (2.0 * DEPTH) ** 0.25
LAMBDA_INIT = 0.8 - 0.6 * math.exp(-0.3 * 0)
LOG2E = math.log2(math.e)
Q_SCALE = HEAD_DIM ** -0.5 * LOG2E

LANES = 128
SUBLANES = 8
BF16_SUBLANES = 16
V7X_VMEM_BYTES = 64 * 1024 * 1024
NEG_BIG = -1e30

BF16 = jnp.bfloat16
F32 = jnp.float32

TM = 1024
TM_SUB = 512
TM_MLP = 1024
MLP_SUB = 256
FF_SPLITS = (0, 1536, 2816)
NA_RB = 16
TQ = 2048
TK = 8192
DA_SUB = 2048
TK_EXACT = 512
DA_EST = 128
ADA_TN = 1536
VMEM_LIMIT = 56 * 1024 * 1024


def _cparams(sem):
    return pltpu.CompilerParams(dimension_semantics=sem, vmem_limit_bytes=VMEM_LIMIT)


def _ln(x):
    mu = jnp.mean(x, axis=-1, keepdims=True)
    xc = x - mu
    var = jnp.mean(xc * xc, axis=-1, keepdims=True)
    return xc * lax.rsqrt(var + LN_EPS)


def _silu(x):
    return x / (1.0 + jnp.exp(-x))


def _ada_kernel(c_ref, w_ref, b_ref, o_ref):
    act = _silu(c_ref[...])
    o_ref[...] = jnp.dot(act, w_ref[...], preferred_element_type=F32) + b_ref[...]


def _ada(c_pad, w, b):
    rows, d = c_pad.shape
    n = w.shape[1]
    return pl.pallas_call(
        _ada_kernel,
        grid=(n // ADA_TN,),
        in_specs=[
            pl.BlockSpec((rows, d), lambda j: (0, 0)),
            pl.BlockSpec((d, ADA_TN), lambda j: (0, j)),
            pl.BlockSpec((1, ADA_TN), lambda j: (0, j)),
        ],
        out_specs=pl.BlockSpec((rows, ADA_TN), lambda j: (0, j)),
        out_shape=jax.ShapeDtypeStruct((rows, n), F32),
        compiler_params=_cparams(("arbitrary",)),
        name="ada",
    )(c_pad, w, b)


def _inv_freq_lanes(shape):
    lane = lax.broadcasted_iota(jnp.int32, shape, len(shape) - 1)
    f = (lane & (HEAD_DIM // 2 - 1)).astype(F32)
    return jnp.exp(f * (-math.log(ROPE_THETA) / (HEAD_DIM // 2)))


def _rope_tab_kernel(cos_ref, sin_ref):
    r = lax.broadcasted_iota(jnp.int32, (TM, LANES), 0).astype(F32)
    ang = r * _inv_freq_lanes((TM, LANES))
    cos_ref[...] = jnp.cos(ang)
    sin_ref[...] = jnp.sin(ang)


def _rope_tab():
    return pl.pallas_call(
        _rope_tab_kernel,
        out_shape=(jax.ShapeDtypeStruct((TM, LANES), F32),) * 2,
        name="rope_tab",
    )()


def _rpb_tab_kernel(rpb_ref, o_ref):
    p = pl.program_id(0)
    w = lax.broadcasted_iota(jnp.int32, (GRID_W, LANES), 0)
    lane = lax.broadcasted_iota(jnp.int32, (GRID_W, LANES), 1)
    c = lane & (GRID_W - 1)
    t = c - w + (NA_KC - 1)
    off = c - jnp.clip(w - NA_KC // 2, 0, GRID_W - NA_KC)
    n_rel = 2 * NA_KC - 1
    n_row = 2 * NA_KR - 1
    even_key_row = lane < GRID_W
    tiles = [[jnp.zeros((GRID_W, LANES), F32) for _ in range(n_row)] for _ in range(2)]
    for tt in range(n_rel):
        hit = t == tt
        for hh in range(2):
            for rr in range(n_row):
                val = rpb_ref[((2 * p + hh) * n_row + rr) * n_rel + tt] * LOG2E
                tiles[hh][rr] = jnp.where(hit, val, tiles[hh][rr])
    for hh in range(2):
        base = [jnp.where(off < NA_KC, jnp.where(off >= 0, x, NEG_BIG), NEG_BIG) for x in tiles[hh]]
        for d in range(NA_KR):
            for a in range(NA_KR // 2):
                lo = 2 * a - d + NA_KR - 1
                o_ref[d, hh * GRID_W:(hh + 1) * GRID_W, a * LANES:(a + 1) * LANES] = (
                    jnp.where(even_key_row, base[lo], base[lo + 1]))


def _rpb_tab(rpb_flat):
    pairs = NA_HEADS // 2
    shape = (NA_KR, pairs, 2 * GRID_W, NA_KR * GRID_W)
    return pl.pallas_call(
        _rpb_tab_kernel,
        grid=(pairs,),
        in_specs=[pl.BlockSpec(memory_space=pltpu.SMEM)],
        out_specs=pl.BlockSpec((NA_KR, None) + shape[2:], lambda p: (0, p, 0, 0)),
        out_shape=jax.ShapeDtypeStruct(shape, F32),
        compiler_params=_cparams(("arbitrary",)),
        name="rpb_tab",
    )(rpb_flat)


def _rope(x, cosv, sin_signed, first_half):
    swapped = jnp.where(first_half, pltpu.roll(x, LANES - HEAD_DIM // 2, 1),
                        pltpu.roll(x, HEAD_DIM // 2, 1))
    return x * cosv + swapped * sin_signed


def _inproj_kernel(x_ref, mod_ref, w_ref, cosb_ref, sinb_ref,
                   naq_ref, nak_ref, nav_ref, daq_ref, dak_ref, davt_ref):
    i = pl.program_id(1)
    shift1 = mod_ref[:, 0:D_MODEL]
    scale1 = mod_ref[:, D_MODEL:2 * D_MODEL]
    o1, o2, o3 = NA_WIDTH, 2 * NA_WIDTH, 3 * NA_WIDTH
    o4, o5 = o3 + DA_QK, o3 + 2 * DA_QK
    base = (i * TM).astype(F32) * _inv_freq_lanes((1, LANES))
    cosa, sina = jnp.cos(base), jnp.sin(base)
    lane = lax.broadcasted_iota(jnp.int32, (TM_SUB, LANES), 1)
    first_half = (lane & (HEAD_DIM - 1)) < HEAD_DIM // 2
    for t in range(TM // TM_SUB):
        rows = slice(t * TM_SUB, (t + 1) * TM_SUB)
        h = _ln(x_ref[rows, :]) * (1.0 + scale1) + shift1
        proj = jnp.dot(h.astype(BF16), w_ref[...].astype(BF16),
                       preferred_element_type=F32)
        naq_ref[rows, :] = (proj[:, :o1] * Q_SCALE).astype(BF16)
        nak_ref[rows, :] = proj[:, o1:o2].astype(BF16)
        nav_ref[rows, :] = proj[:, o2:o3].astype(BF16)
        cosb, sinb = cosb_ref[rows, :], sinb_ref[rows, :]
        cosv = cosa * cosb - sina * sinb
        sinv = sina * cosb + cosa * sinb
        sin_signed = jnp.where(first_half, -sinv, sinv)
        for g in range(DA_QK // LANES):
            sl = slice(g * LANES, (g + 1) * LANES)
            q = _rope(proj[:, o3 + g * LANES:o3 + (g + 1) * LANES], cosv, sin_signed, first_half)
            daq_ref[rows, sl] = (q * Q_SCALE).astype(BF16)
            k = _rope(proj[:, o4 + g * LANES:o4 + (g + 1) * LANES], cosv, sin_signed, first_half)
            dak_ref[rows, sl] = k.astype(BF16)
        davt_ref[:, rows] = proj[:, o5:].T.astype(BF16)


def _inproj(x, mod3, w_in, cosb, sinb):
    b, s, d = x.shape
    tok = lambda width: pl.BlockSpec((None, TM, width), lambda bi, i: (bi, i, 0))
    sds = lambda width: jax.ShapeDtypeStruct((b, s, width), BF16)
    return pl.pallas_call(
        _inproj_kernel,
        grid=(b, s // TM),
        in_specs=[
            tok(d),
            pl.BlockSpec((None, 1, mod3.shape[-1]), lambda bi, i: (bi, 0, 0)),
            pl.BlockSpec((d, IN_COLS), lambda bi, i: (0, 0), pipeline_mode=pl.Buffered(1)),
            pl.BlockSpec((TM, LANES), lambda bi, i: (0, 0)),
            pl.BlockSpec((TM, LANES), lambda bi, i: (0, 0)),
        ],
        out_specs=[tok(NA_WIDTH), tok(NA_WIDTH), tok(NA_WIDTH), tok(DA_QK), tok(DA_QK),
                   pl.BlockSpec((None, DA_WIDTH, TM), lambda bi, i: (bi, 0, i))],
        out_shape=[sds(NA_WIDTH), sds(NA_WIDTH), sds(NA_WIDTH), sds(DA_QK), sds(DA_QK),
                   jax.ShapeDtypeStruct((b, DA_WIDTH, s), BF16)],
        compiler_params=_cparams(("arbitrary", "arbitrary")),
        name="inproj",
    )(x, mod3, w_in, cosb, sinb)


def _nattn_kernel(q_ref, k_ref, v_ref, t_ref, o_ref, *, rows):
    n_keys = NA_KR * GRID_W
    lane = lax.broadcasted_iota(jnp.int32, (GRID_W, LANES), 1)
    first_head = lane < GRID_W
    ones = jnp.ones((n_keys, LANES), BF16)
    for i in range(NA_RB):
        r = pl.program_id(1) * NA_RB + i
        rs = jnp.clip(r - NA_KR // 2, 0, rows - NA_KR)
        d = r - rs
        start = pl.multiple_of(rs * GRID_W, GRID_W)
        tok = slice(i * GRID_W, (i + 1) * GRID_W)
        for p in range(NA_HEADS // 2):
            sl = slice(p * LANES, (p + 1) * LANES)
            q2 = q_ref[tok, sl]
            zero = jnp.zeros_like(q2)
            lhs = jnp.concatenate([jnp.where(first_head, q2, zero),
                                   jnp.where(first_head, zero, q2)], axis=0)
            k2 = k_ref[pl.ds(start, n_keys), sl]
            v2 = v_ref[pl.ds(start, n_keys), sl]
            s = lax.dot_general(lhs, k2, (((1,), (1,)), ((), ())),
                                preferred_element_type=F32)
            s = s + t_ref[d, p]
            e = jnp.exp2(s - jnp.max(s, axis=-1, keepdims=True)).astype(BF16)
            o2 = jnp.dot(e, jnp.concatenate([v2, ones], axis=1),
                         preferred_element_type=F32)
            o2 = o2[:, :LANES] * (1.0 / o2[:, LANES:])
            o_ref[tok, sl] = jnp.where(first_head, o2[:GRID_W], o2[GRID_W:]).astype(o_ref.dtype)


def _nattn(q, k, v, tab):
    b, s, width = q.shape
    rows = s // GRID_W
    once = pl.Buffered(1)
    full = pl.BlockSpec((None, s, width), lambda bi, r: (bi, 0, 0))
    row = pl.BlockSpec((None, NA_RB * GRID_W, width), lambda bi, r: (bi, r, 0))
    return pl.pallas_call(
        functools.partial(_nattn_kernel, rows=rows),
        grid=(b, rows // NA_RB),
        in_specs=[row, full, full,
                  pl.BlockSpec(tab.shape, lambda bi, r: (0, 0, 0, 0), pipeline_mode=once)],
        out_specs=row,
        out_shape=jax.ShapeDtypeStruct((b, s, width), BF16),
        compiler_params=_cparams(("arbitrary", "arbitrary")),
        name="nattn",
    )(q, k, v, tab)


def _da_scores(k_blk, rhs):
    return lax.dot_general(k_blk, rhs, (((1,), (1,)), ((), ())), preferred_element_type=F32)


def _da_output(acc1, acc2, l1, l2, lam, g):
    o = acc1 * (1.0 / l1) - lam * (acc2 * (1.0 / l2))
    ms = jnp.mean(o * o, axis=0, keepdims=True)
    o = o * lax.rsqrt(ms + LN_EPS)
    return o.T * g * (1.0 - LAMBDA_INIT)


def _da_stream(rhs, k_ref, vt_ref, acc_ref, n_keys):
    k0 = k_ref[0:DA_EST, :]
    ref8 = [jnp.broadcast_to(jnp.max(_da_scores(k0, r), axis=0, keepdims=True), (SUBLANES, TQ))
            for r in rhs]
    acc_ref[...] = jnp.zeros_like(acc_ref)

    def body(j, carry):
        out = []
        for mp in range(2):
            l8 = carry[mp]
            pv = None
            for c in range(TK // DA_SUB):
                start = pl.multiple_of(j * TK + c * DA_SUB, DA_SUB)
                s = _da_scores(k_ref[pl.ds(start, DA_SUB), :], rhs[mp])
                p = jnp.exp2(s.reshape(DA_SUB // SUBLANES, SUBLANES, TQ) - ref8[mp])
                l8 = l8 + jnp.sum(p, axis=0)
                d = jnp.dot(vt_ref[:, pl.ds(start, DA_SUB)],
                            p.reshape(DA_SUB, TQ).astype(BF16), preferred_element_type=F32)
                pv = d if pv is None else pv + d
            acc_ref[mp] += pv
            out.append(l8)
        return tuple(out)

    z8 = jnp.zeros((SUBLANES, TQ), F32)
    l8 = lax.fori_loop(0, n_keys // TK, body, (z8, z8))
    return [jnp.sum(l, axis=0, keepdims=True) for l in l8]


def _da_online(rhs, k_ref, vt_ref, acc_ref, n_keys):
    acc_ref[...] = jnp.zeros_like(acc_ref)

    def body(j, carry):
        start = pl.multiple_of(j * TK_EXACT, TK_EXACT)
        kb = k_ref[pl.ds(start, TK_EXACT), :]
        vb = vt_ref[:, pl.ds(start, TK_EXACT)]
        new = []
        for mp in range(2):
            m_old, l_old = carry[2 * mp], carry[2 * mp + 1]
            s = _da_scores(kb, rhs[mp])
            m_new = jnp.maximum(m_old, jnp.max(s, axis=0, keepdims=True))
            a = jnp.exp2(m_old - m_new)
            p = jnp.exp2(s - m_new)
            l_new = a * l_old + jnp.sum(p, axis=0, keepdims=True)
            acc_ref[mp] = acc_ref[mp] * a + jnp.dot(vb, p.astype(BF16),
                                                    preferred_element_type=F32)
            new += [m_new, l_new]
        return tuple(new)

    m0 = jnp.full((1, TQ), -jnp.inf, F32)
    l0 = jnp.zeros((1, TQ), F32)
    _, l1, _, l2 = lax.fori_loop(0, n_keys // TK_EXACT, body, (m0, l0, m0, l0))
    return [l1, l2]


def _dattn_kernel(q_ref, k_ref, vt_ref, lam_ref, g_ref, *rest, n_keys, n_cast):
    w_refs, (o_ref, *wb_refs, acc_ref) = rest[:n_cast], rest[n_cast:]
    for w_ref, wb_ref in zip(w_refs, wb_refs):
        wb_ref[...] = w_ref[...].astype(wb_ref.dtype)
    q = q_ref[...]
    lane = lax.broadcasted_iota(jnp.int32, q.shape, 1)
    zero = jnp.zeros_like(q)
    rhs = (jnp.where(lane < HEAD_DIM, q, zero), jnp.where(lane < HEAD_DIM, zero, q))
    lam = (jnp.exp(jnp.sum(lam_ref[0:1, :] * lam_ref[1:2, :], axis=-1, keepdims=True))
           - jnp.exp(jnp.sum(lam_ref[2:3, :] * lam_ref[3:4, :], axis=-1, keepdims=True))
           + LAMBDA_INIT)

    l1, l2 = _da_stream(rhs, k_ref, vt_ref, acc_ref, n_keys)
    out = _da_output(acc_ref[0], acc_ref[1], l1, l2, lam, g_ref[...])
    o_ref[...] = out.astype(o_ref.dtype)
    overflowed = (jnp.sum(jnp.where(jnp.isfinite(out), 0.0, 1.0))
                  + jnp.sum(jnp.where(jnp.isfinite(l1 + l2), 0.0, 1.0)))

    @pl.when(overflowed > 0.0)
    def _():
        e1, e2 = _da_online(rhs, k_ref, vt_ref, acc_ref, n_keys)
        o_ref[...] = _da_output(acc_ref[0], acc_ref[1], e1, e2, lam,
                                g_ref[...]).astype(o_ref.dtype)


def _cast_spec(w, n_q, n_steps):
    rows = w.shape[0]
    n_blocks = max(n for n in range(1, n_steps + 1)
                   if n_steps % n == 0 and rows % (n * BF16_SUBLANES) == 0)
    rep = n_steps // n_blocks
    return pl.BlockSpec((rows // n_blocks, w.shape[1]),
                        lambda bi, h, i: (((bi * DA_HEADS + h) * n_q + i) // rep, 0))


def _dattn(q, k, vt, lam4, g, weights):
    b, s, _ = q.shape
    n_q = s // TQ
    w_specs = [_cast_spec(w, n_q, b * DA_HEADS * n_q) for w in weights]
    return pl.pallas_call(
        functools.partial(_dattn_kernel, n_keys=s, n_cast=len(weights)),
        grid=(b, DA_HEADS, n_q),
        in_specs=[
            pl.BlockSpec((None, TQ, LANES), lambda bi, h, i: (bi, i, h)),
            pl.BlockSpec((None, s, LANES), lambda bi, h, i: (bi, 0, h)),
            pl.BlockSpec((None, DA_VDIM, s), lambda bi, h, i: (bi, h, 0)),
            pl.BlockSpec(lam4.shape, lambda bi, h, i: (0, 0)),
            pl.BlockSpec(g.shape, lambda bi, h, i: (0, 0)),
        ] + w_specs,
        out_specs=[pl.BlockSpec((None, TQ, DA_VDIM), lambda bi, h, i: (bi, i, h))] + w_specs,
        out_shape=[jax.ShapeDtypeStruct((b, s, DA_WIDTH), BF16)]
        + [jax.ShapeDtypeStruct(w.shape, BF16) for w in weights],
        scratch_shapes=[pltpu.VMEM((2, DA_VDIM, TQ), F32)],
        compiler_params=_cparams(("arbitrary", "arbitrary", "arbitrary")),
        name="dattn",
    )(q, k, vt, lam4, g, *weights)


def _mlp_kernel(oa_ref, ob_ref, x_ref, mod_ref, wo_ref, wg_ref, wu_ref, wd_ref,
                g1_ref, b1_ref, g2_ref, b2_ref, o_ref):
    gate1 = mod_ref[:, 2 * D_MODEL:3 * D_MODEL]
    shift2 = mod_ref[:, 3 * D_MODEL:4 * D_MODEL]
    scale2 = mod_ref[:, 4 * D_MODEL:5 * D_MODEL]
    gate2 = mod_ref[:, 5 * D_MODEL:6 * D_MODEL]
    n_sub = TM_MLP // MLP_SUB
    rows = [slice(t * MLP_SUB, (t + 1) * MLP_SUB) for t in range(n_sub)]

    def attn_out(r):
        return (jnp.dot(oa_ref[r, :], wo_ref[0:NA_WIDTH, :], preferred_element_type=F32)
                + jnp.dot(ob_ref[r, :], wo_ref[NA_WIDTH:, :], preferred_element_type=F32))

    def norm1(r, mix):
        x1 = _ln(ALPHA * x_ref[r, :] + (1.0 + gate1) * mix) * g1_ref[...] + b1_ref[...]
        return x1, (_ln(x1) * (1.0 + scale2) + shift2).astype(BF16)

    def swiglu(h2):
        ffn = None
        for c0, c1 in zip(FF_SPLITS[:-1], FF_SPLITS[1:]):
            gate = jnp.dot(h2, wg_ref[:, c0:c1], preferred_element_type=F32)
            up = jnp.dot(h2, wu_ref[:, c0:c1], preferred_element_type=F32)
            act = (_silu(gate) * up).astype(BF16)
            part = jnp.dot(act, wd_ref[c0:c1, :], preferred_element_type=F32)
            ffn = part if ffn is None else ffn + part
        return ffn

    def norm2(r, x1, ffn):
        o_ref[r, :] = _ln(ALPHA * x1 + (1.0 + gate2) * ffn) * g2_ref[...] + b2_ref[...]

    mix = [attn_out(r) for r in rows]
    x1h2 = [None] * n_sub
    ffn = [None] * n_sub
    x1h2[0] = norm1(rows[0], mix[0])
    for t in range(n_sub):
        ffn[t] = swiglu(x1h2[t][1])
        if t + 1 < n_sub:
            x1h2[t + 1] = norm1(rows[t + 1], mix[t + 1])
        if t > 0:
            norm2(rows[t - 1], x1h2[t - 1][0], ffn[t - 1])
    norm2(rows[-1], x1h2[-1][0], ffn[-1])


def _mlp(oa, ob, x, mod3, w_out, wg, wu, wd, g1, b1, g2, b2):
    b, s, d = x.shape
    assert FF_SPLITS[0] == 0 and FF_SPLITS[-1] == wg.shape[1]
    tok = lambda width: pl.BlockSpec((None, TM_MLP, width), lambda bi, i: (bi, i, 0))
    vec = pl.BlockSpec((1, d), lambda bi, i: (0, 0))
    once = lambda shape: pl.BlockSpec(shape, lambda bi, i: (0, 0), pipeline_mode=pl.Buffered(1))
    return pl.pallas_call(
        _mlp_kernel,
        grid=(b, s // TM_MLP),
        in_specs=[tok(NA_WIDTH), tok(DA_WIDTH), tok(d),
                  pl.BlockSpec((None, 1, mod3.shape[-1]), lambda bi, i: (bi, 0, 0)),
                  once(w_out.shape), once(wg.shape), once(wu.shape), once(wd.shape),
                  vec, vec, vec, vec],
        out_specs=tok(d),
        out_shape=jax.ShapeDtypeStruct((b, s, d), F32),
        compiler_params=_cparams(("arbitrary", "arbitrary")),
        name="mlp",
    )(oa, ob, x, mod3, w_out, wg, wu, wd, g1, b1, g2, b2)


def kernel(x, c, w_ada, b_ada, w_in, rpb, lambda_q1, lambda_k1, lambda_q2, lambda_k2,
           subln_g, w_out, ln1_g, ln1_b, w_gate, w_up, w_down, ln2_g, ln2_b):
    b, s, d = x.shape
    assert d == D_MODEL and s % TM == 0 and s % TM_MLP == 0 and s % GRID_W == 0
    assert s % TQ == 0 and s % TK == 0 and TM % TM_SUB == 0 and TM_MLP % MLP_SUB == 0
    assert s % TK_EXACT == 0 and TK % DA_SUB == 0
    assert w_ada.shape[0] == DEPTH == 1
    l = 0
    sub = 8
    c_pad = jnp.zeros((sub, d), F32).at[:b].set(c)
    mod = _ada(c_pad, w_ada[l], b_ada[l].reshape(1, -1))[:b]
    mod3 = mod.reshape(b, 1, -1)

    cosb, sinb = _rope_tab()
    tab = _rpb_tab(rpb[l].reshape(-1))

    naq, nak, nav, daq, dak, davt = _inproj(x, mod3, w_in[l], cosb, sinb)
    out_a = _nattn(naq, nak, nav, tab)
    lam4 = jnp.stack([lambda_q1[l], lambda_k1[l], lambda_q2[l], lambda_k2[l]])
    out_b, wo, wg, wu, wd = _dattn(daq, dak, davt, lam4, subln_g[l].reshape(1, -1),
                                   (w_out[l], w_gate[l], w_up[l], w_down[l]))
    return _mlp(out_a, out_b, x, mod3, wo, wg, wu, wd,
                ln1_g[l].reshape(1, -1), ln1_b[l].reshape(1, -1),
                ln2_g[l].reshape(1, -1), ln2_b[l].reshape(1, -1))
```
